```python
import jax, jax.numpy as jnp
from jax import lax
import numpy as np

D_MODEL = 2048
BATCH = 16
SEQ = 2048
DEPTH = 4
DEC_BATCH = 8
DEC_SEQ = 4096
PAST_LEN = 128

PLE_DIM = 256
GRID_W = 64
N_MIXERS = 3
N_A_LAYERS = (DEPTH + 2) // 3
N_B_LAYERS = (DEPTH + 1) // 3
N_C_LAYERS = DEPTH // 3
NORM_EPS = 1e-6
NA_HEAD_DIM = 32
NA_HEADS = D_MODEL // NA_HEAD_DIM
NA_WIN_H = 8
NA_WIN_W = 16
NA_QCOLS = 16
NA_KCOLS = NA_QCOLS + NA_WIN_W
POOL_WINDOWS = (2, 4, 8, 16)
POOL_GROUP = D_MODEL // len(POOL_WINDOWS)
GLA_HEADS = 4
GLA_DK = D_MODEL // 2
GLA_DV = D_MODEL
GLA_DK_HEAD = GLA_DK // GLA_HEADS
GLA_DV_HEAD = GLA_DV // GLA_HEADS
GLA_GATE_RANK = 16
GLA_TAU = 16.0
GLA_CHUNK = 64
GLA_IN = 2 * GLA_DK + 2 * GLA_DV + 2 * GLA_GATE_RANK
MOE_GROUPS = 4
MOE_PER_GROUP = 8
MOE_EXPERTS = MOE_GROUPS * MOE_PER_GROUP
MOE_TOPK = 2
MOE_FF = 1024
MOE_BLOCK = 256

kernel_name = 'hybrid_na_pool_gla_hmoe_encoder'

F32 = jnp.float32


def _rmsnorm(x, g):
    xf = x.astype(F32)
    y = xf * lax.rsqrt(jnp.mean(xf * xf, axis=-1, keepdims=True) + NORM_EPS)
    return (y * g.astype(F32)).astype(x.dtype)


def _neighbourhood_attention(h, w_qkv, rpb, w_o):
    B, S, D = h.shape
    rows = S // GRID_W
    kh = min(NA_WIN_H, rows)
    qkv = (h @ w_qkv).reshape(B, rows, GRID_W, 3, NA_HEADS, NA_HEAD_DIM).transpose(3, 0, 4, 1, 2, 5)
    q = qkv[0] * NA_HEAD_DIM ** -0.5
    k, v = qkv[1], qkv[2]
    ncb = GRID_W // NA_QCOLS
    qcol = np.arange(GRID_W).reshape(ncb, NA_QCOLS)
    kstart = np.clip(np.arange(ncb) * NA_QCOLS - NA_WIN_W // 2, 0, GRID_W - NA_KCOLS)
    kcol = kstart[:, None] + np.arange(NA_KCOLS)
    wstart = np.clip(qcol - NA_WIN_W // 2, 0, GRID_W - NA_WIN_W)
    col_ok = (kcol[:, None, :] >= wstart[..., None]) & (kcol[:, None, :] < wstart[..., None] + NA_WIN_W)
    dcol = np.clip(kcol[:, None, :] - qcol[..., None] + NA_WIN_W - 1, 0, 2 * NA_WIN_W - 2)
    col_bias = rpb.astype(F32)[:, :, dcol]
    col_ok = jnp.asarray(col_ok)[:, :, None, :]
    kcol_j = jnp.asarray(kcol)

    def row_block(r):
        rs = jnp.clip(r - kh // 2, 0, rows - kh)
        kb = lax.dynamic_slice_in_dim(k, rs, kh, axis=2)[:, :, :, kcol_j]
        vb = lax.dynamic_slice_in_dim(v, rs, kh, axis=2)[:, :, :, kcol_j]
        qr = lax.dynamic_index_in_dim(q, r, axis=2, keepdims=False).reshape(B, NA_HEADS, ncb, NA_QCOLS, NA_HEAD_DIM)
        s = jnp.einsum('bhnqd,bhknjd->bhnqkj', qr, kb).astype(F32)
        drow = rs + jnp.arange(kh) - r + NA_WIN_H - 1
        bias = jnp.take(col_bias, drow, axis=1).transpose(0, 2, 3, 1, 4)
        s = jnp.where(col_ok, s + bias, -jnp.inf)
        p = jax.nn.softmax(s, axis=(-2, -1)).astype(vb.dtype)
        o = jnp.einsum('bhnqkj,bhknjd->bhnqd', p, vb)
        return o.reshape(B, NA_HEADS, GRID_W, NA_HEAD_DIM)

    o = lax.map(row_block, jnp.arange(rows))
    o = o.transpose(1, 0, 3, 2, 4).reshape(B, S, D)
    return o @ w_o


def _pool_mixer(h, w_pool, scale):
    B, S, D = h.shape
    hf = h.astype(F32)
    csum = jnp.concatenate([jnp.zeros((B, 1, D), F32), jnp.cumsum(hf, axis=1)], axis=1)
    t = jnp.arange(S)
    groups = []
    for g, win in enumerate(POOL_WINDOWS):
        c = slice(g * POOL_GROUP, (g + 1) * POOL_GROUP)
        lo = jnp.clip(t - win // 2, 0, S)
        hi = jnp.clip(t + win // 2, 0, S)
        cg = csum[..., c]
        mean = (cg[:, hi] - cg[:, lo]) / (hi - lo).astype(F32)[None, :, None]
        groups.append(mean - hf[..., c])
    d = jnp.stack(groups, axis=2).astype(h.dtype)
    y = jnp.einsum('bsgc,gcd->bsgd', d, w_pool).reshape(B, S, D)
    return y * scale


def _gla_chunked(q, k, v, log_a):
    B, S, H, dk = q.shape
    dv = v.shape[-1]
    n = S // GLA_CHUNK

    def chunks(t):
        return t.reshape(B, n, GLA_CHUNK, H, t.shape[-1]).transpose(1, 0, 3, 2, 4)

    q, k, v, log_a = chunks(q), chunks(k), chunks(v), chunks(log_a)
    bcum = jnp.cumsum(log_a, axis=3)
    blast = bcum[:, :, :, -1:, :]
    q_in = q * jnp.exp(bcum)
    k_in = k * jnp.exp(-bcum)
    k_end = k * jnp.exp(blast - bcum)
    lower = jnp.tril(jnp.ones((GLA_CHUNK, GLA_CHUNK), bool))
    a = jnp.where(lower, jnp.einsum('nbhid,nbhjd->nbhij', q_in, k_in), 0.0)
    o_intra = jnp.einsum('nbhij,nbhjv->nbhiv', a, v)

    def step(state, xs):
        qc, kc, vc, dec = xs
        o = jnp.einsum('bhid,bhdv->bhiv', qc, state)
        state = dec[:, :, 0, :, None] * state + jnp.einsum('bhjd,bhjv->bhdv', kc, vc)
        return state, o

    s0 = jnp.zeros((B, H, dk, dv), F32)
    _, o_inter = lax.scan(step, s0, (q_in, k_end, v, jnp.exp(blast)))
    o = o_intra + o_inter
    return o.transpose(1, 0, 3, 2, 4).reshape(B, S, H, dv)


def _gla_mixer(h, w_in, w_a2, b_a, g_norm, w_o):
    B, S, D = h.shape
    z = h @ w_in
    cuts = [int(c) for c in np.cumsum([GLA_DK, GLA_DK, GLA_DV, GLA_DV, GLA_GATE_RANK])]
    q, k, v, r, a_f, a_b = jnp.split(z, cuts, axis=-1)

    def heads(t):
        return t.astype(F32).reshape(B, S, GLA_HEADS, -1)

    q = heads(q) * GLA_DK_HEAD ** -0.5
    k = heads(k)
    v = heads(v)

    def log_decay(a, d):
        return heads(jax.nn.log_sigmoid((a @ w_a2[d]).astype(F32) + b_a[d].astype(F32)) / GLA_TAU)

    o_f = _gla_chunked(q, k, v, log_decay(a_f, 0))
    flip = lambda t: jnp.flip(t, axis=1)
    o_b = flip(_gla_chunked(flip(q), flip(k), flip(v), flip(log_decay(a_b, 1))))
    o = o_f + o_b
    o = o * lax.rsqrt(jnp.mean(o * o, axis=-1, keepdims=True) + NORM_EPS) * g_norm.astype(F32)
    o = o.reshape(B, S, D).astype(h.dtype) * jax.nn.silu(r)
    return o @ w_o


def _hier_moe(h, w_rg, b_rg, w_re, b_re, w_gate, w_up, w_down):
    B, S, D = h.shape
    T = B * S
    hf = h.reshape(T, D)
    g_prob = jax.nn.softmax((hf @ w_rg).astype(F32) + b_rg.astype(F32), axis=-1)
    g_top, g_idx = lax.top_k(g_prob, 1)
    e_logits = ((hf @ w_re).astype(F32) + b_re.astype(F32)).reshape(T, MOE_GROUPS, MOE_PER_GROUP)
    e_logits = e_logits[jnp.arange(T), g_idx[:, 0]]
    e_top, e_idx = lax.top_k(jax.nn.softmax(e_logits, axis=-1), MOE_TOPK)
    gate = g_top * e_top / jnp.sum(e_top, axis=-1, keepdims=True)
    flat_e = (g_idx * MOE_PER_GROUP + e_idx).reshape(-1)
    flat_tok = jnp.repeat(jnp.arange(T, dtype=jnp.int32), MOE_TOPK)
    flat_w = gate.reshape(-1)
    n_assign = T * MOE_TOPK
    n_blocks = -(-n_assign // MOE_BLOCK) + MOE_EXPERTS
    n_rows = n_blocks * MOE_BLOCK
    order = jnp.argsort(flat_e)
    e_sorted = flat_e[order]
    counts = jnp.bincount(flat_e, length=MOE_EXPERTS)
    padded = (counts + MOE_BLOCK - 1) // MOE_BLOCK * MOE_BLOCK
    pad_end = jnp.cumsum(padded)
    dest = (pad_end - padded)[e_sorted] + jnp.arange(n_assign) - (jnp.cumsum(counts) - counts)[e_sorted]
    row_tok = jnp.full((n_rows,), T, jnp.int32).at[dest].set(flat_tok[order])
    row_w = jnp.zeros((n_rows,), F32).at[dest].set(flat_w[order])
    blk_e = jnp.minimum(jnp.searchsorted(pad_end, jnp.arange(n_blocks) * MOE_BLOCK, side='right'), MOE_EXPERTS - 1)
    x_pad = jnp.concatenate([hf, jnp.zeros((1, D), hf.dtype)], axis=0)

    def expert_block(args):
        e, tok, w = args
        xb = x_pad[tok]
        a = xb @ w_gate[e]
        u = xb @ w_up[e]
        return ((jax.nn.silu(a) * u) @ w_down[e]) * w[:, None].astype(xb.dtype)

    ys = lax.map(expert_block, (blk_e, row_tok.reshape(n_blocks, MOE_BLOCK), row_w.reshape(n_blocks, MOE_BLOCK)))
    y = jnp.zeros((T + 1, D), ys.dtype).at[row_tok].add(ys.reshape(n_rows, D))[:T]
    return y.reshape(B, S, D).astype(h.dtype)


def _trunk(x, p, norm_mix, norm_ffn, norm_ple, norm_final, na_w_qkv, na_rpb, na_w_o, pool_w, pool_scale,
           gla_w_in, gla_w_a2, gla_b_a, gla_norm, gla_w_o, moe_w_rg, moe_b_rg, moe_w_re, moe_b_re,
           moe_w_gate, moe_w_up, moe_w_down, ple_w_proj, ple_w_gate):
    for i in range(DEPTH):
        h = _rmsnorm(x, norm_mix[i])
        j = i // N_MIXERS
        kind = i % N_MIXERS
        if kind == 0:
            mix = _neighbourhood_attention(h, na_w_qkv[j], na_rpb[j], na_w_o[j])
        elif kind == 1:
            mix = _pool_mixer(h, pool_w[j], pool_scale[j])
        else:
            mix = _gla_mixer(h, gla_w_in[j], gla_w_a2[j], gla_b_a[j], gla_norm[j], gla_w_o[j])
        x = x + mix
        x = x + _hier_moe(_rmsnorm(x, norm_ffn[i]), moe_w_rg[i], moe_b_rg[i], moe_w_re[i], moe_b_re[i],
                          moe_w_gate[i], moe_w_up[i], moe_w_down[i])
        hp = _rmsnorm(x, norm_ple[i])
        x = x + jax.nn.sigmoid(hp @ ple_w_gate[i]) * (p[i] @ ple_w_proj[i])
    return _rmsnorm(x, norm_final)


def setup_inputs(seed: int = 0) -> dict:
    key = jax.random.key(seed)
    ks = iter(list(jax.random.split(key, 32)))
    D = D_MODEL

    def nrm(shape, scale):
        return jax.random.normal(next(ks), shape, F32) * scale

    def gain(shape):
        return 1.0 + 0.02 * jax.random.normal(next(ks), shape, F32)

    return {
        'x_prompt': nrm((BATCH, SEQ, D), 1.0),
        'x_sample': nrm((DEC_BATCH, DEC_SEQ, D), 1.0),
        'p_prompt': nrm((DEPTH, BATCH, SEQ, PLE_DIM), 1.0),
        'p_sample': nrm((DEPTH, DEC_BATCH, DEC_SEQ, PLE_DIM), 1.0),
        'norm_mix': gain((DEPTH, D)),
        'norm_ffn': gain((DEPTH, D)),
        'norm_ple': gain((DEPTH, D)),
        'norm_final': gain((D,)),
        'na_w_qkv': nrm((N_A_LAYERS, D, 3 * D), D ** -0.5),
        'na_rpb': nrm((N_A_LAYERS, NA_HEADS, 2 * NA_WIN_H - 1, 2 * NA_WIN_W - 1), 0.1),
        'na_w_o': nrm((N_A_LAYERS, D, D), D ** -0.5),
        'pool_w': nrm((N_B_LAYERS, len(POOL_WINDOWS), POOL_GROUP, POOL_GROUP), POOL_GROUP ** -0.5),
        'pool_scale': gain((N_B_LAYERS, D)),
        'gla_w_in': nrm((N_C_LAYERS, D, GLA_IN), D ** -0.5),
        'gla_w_a2': nrm((N_C_LAYERS, 2, GLA_GATE_RANK, GLA_DK), GLA_GATE_RANK ** -0.5),
        'gla_b_a': nrm((N_C_LAYERS, 2, GLA_DK), 0.1),
        'gla_norm': gain((N_C_LAYERS, GLA_DV_HEAD)),
        'gla_w_o': nrm((N_C_LAYERS, D, D), D ** -0.5),
        'moe_w_rg': nrm((DEPTH, D, MOE_GROUPS), D ** -0.5),
        'moe_b_rg': nrm((DEPTH, MOE_GROUPS), 0.01),
        'moe_w_re': nrm((DEPTH, D, MOE_EXPERTS), D ** -0.5),
        'moe_b_re': nrm((DEPTH, MOE_EXPERTS), 0.01),
        'moe_w_gate': nrm((DEPTH, MOE_EXPERTS, D, MOE_FF), D ** -0.5),
        'moe_w_up': nrm((DEPTH, MOE_EXPERTS, D, MOE_FF), D ** -0.5),
        'moe_w_down': nrm((DEPTH, MOE_EXPERTS, MOE_FF, D), MOE_FF ** -0.5),
        'ple_w_proj': nrm((DEPTH, PLE_DIM, D), PLE_DIM ** -0.5),
        'ple_w_gate': nrm((DEPTH, D, D), D ** -0.5),
    }


def reference(x_prompt, x_sample, p_prompt, p_sample, norm_mix, norm_ffn, norm_ple, norm_final,
              na_w_qkv, na_rpb, na_w_o, pool_w, pool_scale, gla_w_in, gla_w_a2, gla_b_a, gla_norm, gla_w_o,
              moe_w_rg, moe_b_rg, moe_w_re, moe_b_re, moe_w_gate, moe_w_up, moe_w_down, ple_w_proj, ple_w_gate):
    shared = (norm_mix, norm_ffn, norm_ple, norm_final, na_w_qkv, na_rpb, na_w_o, pool_w, pool_scale,
              gla_w_in, gla_w_a2, gla_b_a, gla_norm, gla_w_o, moe_w_rg, moe_b_rg, moe_w_re, moe_b_re,
              moe_w_gate, moe_w_up, moe_w_down, ple_w_proj, ple_w_gate)
    y_prompt = _trunk(x_prompt, p_prompt, *shared)
    y_sample = _trunk(x_sample, p_sample, *shared)
    return (y_prompt, y_sample)
```

```python
import functools

import numpy as np
import jax
import jax.numpy as jnp
from jax import lax
from jax.experimental import pallas as pl
from jax.experimental.pallas import tpu as pltpu
from jax.experimental.pallas import tpu_sc as plsc

F32 = jnp.float32
BF16 = jnp.bfloat16
I32 = jnp.int32
U32 = jnp.uint32

NORM_EPS = 1e-6
GRID_W = 64
NA_HEAD_DIM = 32
NA_WIN_H = 8
NA_WIN_W = 16
NA_HEADS_PER_STEP = 4
POOL_WINDOWS = (2, 4, 8, 16)
POOL_HALO = 16
GLA_HEADS = 4
GLA_GATE_RANK = 16
GLA_TAU = 16.0
GLA_CHUNK = 64
MOE_GROUPS = 4
MOE_PER_GROUP = 8
MOE_EXPERTS = MOE_GROUPS * MOE_PER_GROUP
MOE_ROW_BLOCK = 512
LANES = 128
SC_CORES = 2
SC_WORKERS = SC_CORES * 16
SC_CHUNK = 32
VMEM_LIMIT = 56 * 1024 * 1024


def _tile(n, pref):
    t = min(n, pref)
    assert n % t == 0, (n, pref)
    return t


def _params(sem, vmem=VMEM_LIMIT):
    return pltpu.CompilerParams(dimension_semantics=sem, vmem_limit_bytes=vmem)


def _rms(x, g):
    ms = jnp.mean(x * x, axis=-1, keepdims=True)
    return x * lax.rsqrt(ms + NORM_EPS) * g


def _norm_mm_kernel(x_ref, g_ref, w_ref, cs_ref, o_ref, h_ref):
    @pl.when(pl.program_id(1) == 0)
    def _():
        h_ref[...] = _rms(x_ref[...], g_ref[...]).astype(BF16)

    acc = jnp.dot(h_ref[...], w_ref[...], preferred_element_type=F32)
    o_ref[...] = (acc * cs_ref[...]).astype(o_ref.dtype)


def norm_matmul(x, gain, w, colscale, out_dtype=BF16, tm=1024, tn=512):
    T, D = x.shape
    N = w.shape[1]
    tm, tn = _tile(T, tm), _tile(N, tn)
    return pl.pallas_call(
        _norm_mm_kernel,
        grid=(T // tm, N // tn),
        in_specs=[pl.BlockSpec((tm, D), lambda i, j: (i, 0)),
                  pl.BlockSpec((1, D), lambda i, j: (0, 0)),
                  pl.BlockSpec((D, tn), lambda i, j: (0, j)),
                  pl.BlockSpec((1, tn), lambda i, j: (0, j))],
        out_specs=pl.BlockSpec((tm, tn), lambda i, j: (i, j)),
        out_shape=jax.ShapeDtypeStruct((T, N), out_dtype),
        scratch_shapes=[pltpu.VMEM((tm, D), BF16)],
        compiler_params=_params(("parallel", "arbitrary")),
        name="norm_matmul",
    )(x, gain.reshape(1, D), w, colscale.reshape(1, N))


def _mm_res_kernel(a_ref, w_ref, r_ref, o_ref):
    o_ref[...] = r_ref[...] + jnp.dot(a_ref[...], w_ref[...], preferred_element_type=F32)


def matmul_residual(a, w, res, tm=1024, tn=512):
    T, K = a.shape
    N = w.shape[1]
    tm, tn = _tile(T, tm), _tile(N, tn)
    return pl.pallas_call(
        _mm_res_kernel,
        grid=(T // tm, N // tn),
        in_specs=[pl.BlockSpec((tm, K), lambda i, j: (i, 0)),
                  pl.BlockSpec((K, tn), lambda i, j: (0, j)),
                  pl.BlockSpec((tm, tn), lambda i, j: (i, j))],
        out_specs=pl.BlockSpec((tm, tn), lambda i, j: (i, j)),
        out_shape=jax.ShapeDtypeStruct((T, N), F32),
        compiler_params=_params(("parallel", "arbitrary")),
        name="matmul_residual",
    )(a, w, res)


def _na_bias_table(rpb):
    H = rpb.shape[0]
    q = np.arange(GRID_W)[:, None]
    c = np.arange(GRID_W)[None, :]
    wstart = np.clip(q - NA_WIN_W // 2, 0, GRID_W - NA_WIN_W)
    ok = (c >= wstart) & (c < wstart + NA_WIN_W)
    dcol = np.clip(c - q + NA_WIN_W - 1, 0, 2 * NA_WIN_W - 2)
    dl = np.arange(NA_WIN_H)[:, None]
    k = np.arange(NA_WIN_H)[None, :]
    drow = k - dl + NA_WIN_H - 1
    b = rpb.astype(F32)[:, drow][:, :, :, dcol]
    b = jnp.where(jnp.asarray(ok)[None, None, None], b, -jnp.inf)
    b = b.transpose(0, 1, 3, 2, 4)
    b = b.reshape(H // NA_HEADS_PER_STEP, NA_HEADS_PER_STEP, NA_WIN_H, GRID_W, NA_WIN_H * GRID_W)
    b = b.transpose(0, 2, 1, 3, 4)
    return b.reshape(H // NA_HEADS_PER_STEP, NA_WIN_H, NA_HEADS_PER_STEP * GRID_W, NA_WIN_H * GRID_W)


def _na_kernel(q_ref, k_ref, v_ref, b_ref, o_ref, *, rows):
    nkeys = NA_WIN_H * GRID_W
    lane_head = lax.broadcasted_iota(I32, (GRID_W, LANES), 1) // NA_HEAD_DIM
    masks = [lane_head == h for h in range(NA_HEADS_PER_STEP)]

    def body(r, carry):
        rs = jnp.clip(r - NA_WIN_H // 2, 0, rows - NA_WIN_H)
        q = q_ref[pl.ds(pl.multiple_of(r * GRID_W, GRID_W), GRID_W), :]
        kk = k_ref[pl.ds(pl.multiple_of(rs * GRID_W, GRID_W), nkeys), :]
        vv = v_ref[pl.ds(pl.multiple_of(rs * GRID_W, GRID_W), nkeys), :]
        zero = jnp.zeros_like(q)
        qm = jnp.concatenate([jnp.where(m, q, zero) for m in masks], axis=0)
        s = lax.dot_general(qm, kk, (((1,), (1,)), ((), ())), preferred_element_type=F32)
        s = s + b_ref[r - rs]
        m = jnp.max(s, axis=-1, keepdims=True)
        p = jnp.exp(s - m)
        l = jnp.sum(p, axis=-1, keepdims=True)
        pv = jnp.dot(p.astype(BF16), vv, preferred_element_type=F32) / l
        o = jnp.zeros((GRID_W, LANES), F32)
        for h in range(NA_HEADS_PER_STEP):
            o = jnp.where(masks[h], pv[h * GRID_W:(h + 1) * GRID_W], o)
        o_ref[pl.ds(pl.multiple_of(r * GRID_W, GRID_W), GRID_W), :] = o.astype(o_ref.dtype)
        return carry

    lax.fori_loop(0, rows, body, 0)


def na_attention(qkv, bias, B, S):
    T, D3 = qkv.shape
    D = D3 // 3
    G = D // LANES
    rows = S // GRID_W
    assert rows >= NA_WIN_H and S % GRID_W == 0
    nq = NA_HEADS_PER_STEP * GRID_W
    nkeys = NA_WIN_H * GRID_W
    return pl.pallas_call(
        functools.partial(_na_kernel, rows=rows),
        grid=(G, B),
        in_specs=[pl.BlockSpec((S, LANES), lambda g, b: (b, g)),
                  pl.BlockSpec((S, LANES), lambda g, b: (b, G + g)),
                  pl.BlockSpec((S, LANES), lambda g, b: (b, 2 * G + g)),
                  pl.BlockSpec((None, NA_WIN_H, nq, nkeys), lambda g, b: (g, 0, 0, 0))],
        out_specs=pl.BlockSpec((S, LANES), lambda g, b: (b, g)),
        out_shape=jax.ShapeDtypeStruct((T, D), BF16),
        compiler_params=_params(("parallel", "parallel")),
        name="na_attention",
    )(qkv, qkv, qkv, bias)


def _pool_kernel(xp_ref, xc_ref, xn_ref, g_ref, w_ref, sc_ref, o_ref, h_ref, *, S, tm):
    i = pl.program_id(1)
    nt = pl.num_programs(1)
    g = g_ref[...]
    D = xc_ref.shape[1]
    dg = D // len(POOL_WINDOWS)
    xc = xc_ref[...]
    hc = _rms(xc, g)
    h_ref[pl.ds(0, POOL_HALO), :] = jnp.where(i > 0, _rms(xp_ref[...], g), 0.0)
    h_ref[pl.ds(POOL_HALO, tm), :] = hc
    h_ref[pl.ds(POOL_HALO + tm, POOL_HALO), :] = jnp.where(i < nt - 1, _rms(xn_ref[...], g), 0.0)
    t = i * tm + lax.broadcasted_iota(I32, (tm, 1), 0)
    for gi, win in enumerate(POOL_WINDOWS):
        c0 = gi * dg
        acc = h_ref[pl.ds(POOL_HALO - win // 2, tm), pl.ds(c0, dg)]
        for o in range(-win // 2 + 1, win // 2):
            acc = acc + h_ref[pl.ds(POOL_HALO + o, tm), pl.ds(c0, dg)]
        cnt = jnp.minimum(t + win // 2, S) - jnp.maximum(t - win // 2, 0)
        d = acc / cnt.astype(F32) - hc[:, c0:c0 + dg]
        y = jnp.dot(d.astype(BF16), w_ref[gi], preferred_element_type=F32)
        o_ref[:, pl.ds(c0, dg)] = xc[:, c0:c0 + dg] + y * sc_ref[:, pl.ds(c0, dg)]


def pool_mixer(x, gain, w_pool, scale, B, S, tm=256):
    T, D = x.shape
    tm = _tile(S, tm)
    nt = S // tm
    hb = tm // POOL_HALO
    nhb = S // POOL_HALO
    return pl.pallas_call(
        functools.partial(_pool_kernel, S=S, tm=tm),
        grid=(B, nt),
        in_specs=[pl.BlockSpec((POOL_HALO, D), lambda b, i: (b * nhb + jnp.maximum(i * hb - 1, 0), 0)),
                  pl.BlockSpec((tm, D), lambda b, i: (b * nt + i, 0)),
                  pl.BlockSpec((POOL_HALO, D),
                               lambda b, i: (b * nhb + jnp.minimum((i + 1) * hb, nhb - 1), 0)),
                  pl.BlockSpec((1, D), lambda b, i: (0, 0)),
                  pl.BlockSpec(w_pool.shape, lambda b, i: (0, 0, 0)),
                  pl.BlockSpec((1, D), lambda b, i: (0, 0))],
        out_specs=pl.BlockSpec((tm, D), lambda b, i: (b * nt + i, 0)),
        out_shape=jax.ShapeDtypeStruct((T, D), F32),
        scratch_shapes=[pltpu.VMEM((tm + 2 * POOL_HALO, D), F32)],
        compiler_params=_params(("parallel", "arbitrary")),
        name="pool_mixer",
    )(x, x, x, gain.reshape(1, D), w_pool, scale.reshape(1, D))


def _gla_kernel(*refs, reverse, finalize, nchunks):
    if finalize:
        (q_ref, k_ref, v_ref, a_ref, wa_ref, ba_ref, of_ref, r_ref, gn_ref, o_ref, st_ref) = refs
    else:
        (q_ref, k_ref, v_ref, a_ref, wa_ref, ba_ref, o_ref, st_ref) = refs
    C = GLA_CHUNK

    @pl.when(pl.program_id(2) == 0)
    def _():
        st_ref[...] = jnp.zeros_like(st_ref)

    ii = lax.broadcasted_iota(I32, (C, C), 0)
    jj = lax.broadcasted_iota(I32, (C, C), 1)
    causal = (ii <= jj) if reverse else (ii >= jj)
    tri = causal.astype(F32)
    a_lo = GLA_GATE_RANK if reverse else 0
    wa = wa_ref[...]
    ba = ba_ref[...]

    def body(ci, carry):
        c = (nchunks - 1 - ci) if reverse else ci
        sl = pl.ds(pl.multiple_of(c * C, C), C)
        a = a_ref[sl, :][:, a_lo:a_lo + GLA_GATE_RANK].astype(F32)
        logit = jnp.dot(a, wa, precision=lax.Precision.HIGHEST, preferred_element_type=F32) + ba
        log_sig = jnp.minimum(logit, 0.0) - jnp.log1p(jnp.exp(-jnp.abs(logit)))
        la = log_sig * (1.0 / GLA_TAU)
        bc = jnp.dot(tri, la, precision=lax.Precision.HIGHEST, preferred_element_type=F32)
        bl = jnp.sum(la, axis=0, keepdims=True)
        q = q_ref[sl, :].astype(F32)
        k = k_ref[sl, :].astype(F32)
        v = v_ref[sl, :]
        q_in = (q * jnp.exp(bc)).astype(BF16)
        k_in = (k * jnp.exp(-bc)).astype(BF16)
        k_end = (k * jnp.exp(bl - bc)).astype(BF16)
        att = lax.dot_general(q_in, k_in, (((1,), (1,)), ((), ())), preferred_element_type=F32)
        att = jnp.where(causal, att, 0.0).astype(BF16)
        st = st_ref[...]
        o = jnp.dot(att, v, preferred_element_type=F32)
        o = o + lax.dot_general(q_in, st.astype(BF16), (((1,), (1,)), ((), ())),
                                preferred_element_type=F32)
        st_ref[...] = st * jnp.exp(bl) + lax.dot_general(
            v, k_end, (((0,), (0,)), ((), ())), preferred_element_type=F32)
        if finalize:
            o = o + of_ref[sl, :].astype(F32)
            o = o * lax.rsqrt(jnp.mean(o * o, axis=-1, keepdims=True) + NORM_EPS) * gn_ref[...]
            r = r_ref[sl, :].astype(F32)
            o = o * (r * jax.nn.sigmoid(r))
        o_ref[sl, :] = o.astype(o_ref.dtype)
        return carry

    lax.fori_loop(0, nchunks, body, 0)


def _gla_pass(z, a, w_a2, b_a, B, S, D, *, reverse, o_fwd=None, g_norm=None, tb=512):
    T = z.shape[0]
    H = GLA_HEADS
    dk = D // 2 // H
    dv = D // H
    tb = _tile(S, tb)
    nb = S // tb
    finalize = o_fwd is not None
    d = 1 if reverse else 0

    def rb(b, i):
        return b * nb + ((nb - 1 - i) if reverse else i)

    in_specs = [pl.BlockSpec((tb, dk), lambda b, h, i: (rb(b, i), h)),
                pl.BlockSpec((tb, dk), lambda b, h, i: (rb(b, i), H + h)),
                pl.BlockSpec((tb, dv), lambda b, h, i: (rb(b, i), H + h)),
                pl.BlockSpec((tb, 2 * GLA_GATE_RANK), lambda b, h, i: (rb(b, i), 0)),
                pl.BlockSpec((None, GLA_GATE_RANK, dk), lambda b, h, i: (d, 0, h)),
                pl.BlockSpec((None, 1, dk), lambda b, h, i: (d, 0, h))]
    args = [z, z, z, a, w_a2, b_a.reshape(2, 1, D // 2)]
    if finalize:
        in_specs += [pl.BlockSpec((tb, dv), lambda b, h, i: (rb(b, i), h)),
                     pl.BlockSpec((tb, dv), lambda b, h, i: (rb(b, i), 2 * H + h)),
                     pl.BlockSpec((1, dv), lambda b, h, i: (0, 0))]
        args += [o_fwd, z, g_norm.reshape(1, dv)]
    return pl.pallas_call(
        functools.partial(_gla_kernel, reverse=reverse, finalize=finalize, nchunks=tb // GLA_CHUNK),
        grid=(B, H, nb),
        in_specs=in_specs,
        out_specs=pl.BlockSpec((tb, dv), lambda b, h, i: (rb(b, i), h)),
        out_shape=jax.ShapeDtypeStruct((T, D), BF16),
        scratch_shapes=[pltpu.VMEM((dv, dk), F32)],
        compiler_params=_params(("parallel", "parallel", "arbitrary")),
        name="gla_bwd_finalize" if finalize else "gla_fwd",
    )(*args)


def gla_mixer_core(z, a, w_a2, b_a, g_norm, B, S, D):
    o_f = _gla_pass(z, a, w_a2, b_a, B, S, D, reverse=False)
    return _gla_pass(z, a, w_a2, b_a, B, S, D, reverse=True, o_fwd=o_f, g_norm=g_norm)


def _pack_bf16_pairs(x):
    m = x.shape[1] // 2
    xb = x.astype(BF16).astype(F32)
    lo = pltpu.bitcast(xb[:, :m], U32) >> 16
    hi = pltpu.bitcast(xb[:, m:], U32)
    return hi | lo


def _unpack_bf16_pairs(p):
    lo = pltpu.bitcast(p << 16, F32)
    hi = pltpu.bitcast(p & jnp.uint32(0xFFFF0000), F32)
    return lo, hi


def _router_kernel(x_ref, g_ref, w_ref, b_ref, hp_ref, ri_ref, rw_ref):
    h = _rms(x_ref[...], g_ref[...])
    hp_ref[...] = _pack_bf16_pairs(h)
    logits = jnp.dot(h, w_ref[...], precision=lax.Precision.HIGHEST,
                     preferred_element_type=F32) + b_ref[...]
    lane = lax.broadcasted_iota(I32, logits.shape, 1)
    lane_f = lane.astype(F32)
    neg = -jnp.inf

    def first_lane(hit):
        return jnp.min(jnp.where(hit, lane_f, float(LANES)), axis=-1, keepdims=True).astype(I32)

    gl = jnp.where(lane < MOE_GROUPS, logits, neg)
    gm = jnp.max(gl, axis=-1, keepdims=True)
    g_top = 1.0 / jnp.sum(jnp.exp(gl - gm), axis=-1, keepdims=True)
    g_idx = first_lane(gl == gm)
    e_lane = lane - MOE_GROUPS
    in_grp = (e_lane >= 0) & (e_lane < MOE_EXPERTS) & ((e_lane >> 3) == g_idx)
    el = jnp.where(in_grp, logits, neg)
    m1 = jnp.max(el, axis=-1, keepdims=True)
    es = jnp.sum(jnp.exp(el - m1), axis=-1, keepdims=True)
    i1 = first_lane(el == m1)
    el2 = jnp.where(lane == i1, neg, el)
    m2 = jnp.max(el2, axis=-1, keepdims=True)
    i2 = first_lane(el2 == m2)
    p1 = 1.0 / es
    p2 = jnp.exp(m2 - m1) / es
    w1 = g_top * p1 / (p1 + p2)
    w2 = g_top * p2 / (p1 + p2)
    ri_ref[...] = jnp.where(lane == 0, i1 - MOE_GROUPS, jnp.where(lane == 1, i2 - MOE_GROUPS, 0))
    rw_ref[...] = jnp.where(lane == 0, w1, jnp.where(lane == 1, w2, 0.0))


def moe_router(x, gain, w_router, b_router, tm=512):
    T, D = x.shape
    tm = _tile(T, tm)
    return pl.pallas_call(
        _router_kernel,
        grid=(T // tm,),
        in_specs=[pl.BlockSpec((tm, D), lambda i: (i, 0)),
                  pl.BlockSpec((1, D), lambda i: (0, 0)),
                  pl.BlockSpec((D, LANES), lambda i: (0, 0)),
                  pl.BlockSpec((1, LANES), lambda i: (0, 0))],
        out_specs=[pl.BlockSpec((tm, D // 2), lambda i: (i, 0)),
                   pl.BlockSpec((tm, LANES), lambda i: (i, 0)),
                   pl.BlockSpec((tm, LANES), lambda i: (i, 0))],
        out_shape=[jax.ShapeDtypeStruct((T, D // 2), U32),
                   jax.ShapeDtypeStruct((T, LANES), I32),
                   jax.ShapeDtypeStruct((T, LANES), F32)],
        compiler_params=_params(("parallel",)),
        name="moe_router",
    )(x, gain.reshape(1, D), w_router, b_router)


def _expert_kernel(be_ref, bv_ref, x_ref, wg_ref, wu_ref, wd_ref, o_ref):
    i = pl.program_id(0)
    nvalid = bv_ref[i]

    @pl.when(nvalid > 0)
    def _():
        row = lax.broadcasted_iota(I32, x_ref.shape, 0)
        xp = jnp.where(row < nvalid, x_ref[...], jnp.uint32(0))
        lo, hi = _unpack_bf16_pairs(xp)
        x = jnp.concatenate([lo.astype(BF16), hi.astype(BF16)], axis=1)
        a = jnp.dot(x, wg_ref[...], preferred_element_type=F32)
        u = jnp.dot(x, wu_ref[...], preferred_element_type=F32)
        hmid = (a * jax.nn.sigmoid(a) * u).astype(BF16)
        y = jnp.dot(hmid, wd_ref[...], preferred_element_type=F32)
        o_ref[...] = _pack_bf16_pairs(y)

    @pl.when(nvalid == 0)
    def _():
        o_ref[...] = jnp.zeros_like(o_ref)


def moe_experts(xs, blk_e, blk_valid, w_gate, w_up, w_down):
    n_rows, dh = xs.shape
    E, D, FF = w_gate.shape
    bm = MOE_ROW_BLOCK
    nblk = n_rows // bm
    grid_spec = pltpu.PrefetchScalarGridSpec(
        num_scalar_prefetch=2,
        grid=(nblk,),
        in_specs=[pl.BlockSpec((bm, dh), lambda i, be, bv: (i, 0)),
                  pl.BlockSpec((None, D, FF), lambda i, be, bv: (be[i], 0, 0)),
                  pl.BlockSpec((None, D, FF), lambda i, be, bv: (be[i], 0, 0)),
                  pl.BlockSpec((None, FF, D), lambda i, be, bv: (be[i], 0, 0))],
        out_specs=pl.BlockSpec((bm, dh), lambda i, be, bv: (i, 0)),
    )
    return pl.pallas_call(
        _expert_kernel,
        grid_spec=grid_spec,
        out_shape=jax.ShapeDtypeStruct((n_rows, dh), U32),
        compiler_params=_params(("arbitrary",)),
        name="moe_experts",
    )(blk_e, blk_valid, xs, w_gate, w_up, w_down)


def _sc_mesh():
    return plsc.VectorSubcoreMesh(core_axis_name="c", subcore_axis_name="s")


def _sc_worker_id():
    return lax.axis_index("s") * SC_CORES + lax.axis_index("c")


def sc_scatter_rows(src, pos, n_rows):
    T, W = src.shape
    per_w = T // SC_WORKERS
    nch = per_w // SC_CHUNK
    assert per_w * SC_WORKERS == T and nch * SC_CHUNK == per_w
    idx = pos.reshape(SC_WORKERS, nch, SC_CHUNK, 2).transpose(0, 1, 3, 2).reshape(SC_WORKERS, 2 * nch, SC_CHUNK)

    @functools.partial(
        pl.kernel, mesh=_sc_mesh(),
        out_type=jax.ShapeDtypeStruct((n_rows, W), src.dtype),
        scratch_types=[pltpu.VMEM((2 * nch, SC_CHUNK), I32),
                       pltpu.VMEM((SC_CHUNK, W), src.dtype),
                       pltpu.SemaphoreType.DMA],
    )
    def k(src_hbm, idx_hbm, out_hbm, idx_v, rows_v, sem):
        wid = _sc_worker_id()
        base = wid * per_w
        pltpu.sync_copy(idx_hbm.at[wid], idx_v)

        @pl.loop(0, nch)
        def _(c):
            pltpu.sync_copy(src_hbm.at[pl.ds(base + c * SC_CHUNK, SC_CHUNK)], rows_v)
            pltpu.async_copy(rows_v, out_hbm.at[idx_v.at[2 * c]], sem).wait()
            pltpu.async_copy(rows_v, out_hbm.at[idx_v.at[2 * c + 1]], sem).wait()

    return k(src, idx)


def sc_gather_rows(table, idx_flat):
    N = idx_flat.shape[0]
    W = table.shape[1]
    per_w = N // SC_WORKERS
    nch = per_w // SC_CHUNK
    assert per_w * SC_WORKERS == N and nch * SC_CHUNK == per_w
    idx = idx_flat.reshape(SC_WORKERS, nch, SC_CHUNK)

    @functools.partial(
        pl.kernel, mesh=_sc_mesh(),
        out_type=jax.ShapeDtypeStruct((N, W), table.dtype),
        scratch_types=[pltpu.VMEM((nch, SC_CHUNK), I32),
                       pltpu.VMEM((SC_CHUNK, W), table.dtype),
                       pltpu.SemaphoreType.DMA],
    )
    def k(table_hbm, idx_hbm, out_hbm, idx_v, rows_v, sem):
        wid = _sc_worker_id()
        base = wid * per_w
        pltpu.sync_copy(idx_hbm.at[wid], idx_v)

        @pl.loop(0, nch)
        def _(c):
            pltpu.async_copy(table_hbm.at[idx_v.at[c]], rows_v, sem).wait()
            pltpu.sync_copy(rows_v, out_hbm.at[pl.ds(base + c * SC_CHUNK, SC_CHUNK)])

    return k(table, idx)


def _dispatch_plan(eidx, bm):
    T = eidx.shape[0]
    E = MOE_EXPERTS
    n_rows = (T * 2 // bm + E) * bm
    nblk = n_rows // bm
    onehot = (eidx[:, :, None] == jnp.arange(E, dtype=I32)).astype(I32).sum(axis=1)
    incl = jnp.cumsum(onehot, axis=0)
    counts = incl[-1]
    rank = incl - onehot
    padded = (counts + bm - 1) // bm * bm
    pad_end = jnp.cumsum(padded)
    base = pad_end - padded
    pos = base[eidx] + jnp.take_along_axis(rank, eidx, axis=1)
    blk_start = jnp.arange(nblk, dtype=I32) * bm
    blk_e = jnp.minimum(jnp.searchsorted(pad_end, blk_start, side="right"), E - 1).astype(I32)
    blk_valid = jnp.clip(counts[blk_e] - (blk_start - base[blk_e]), 0, bm)
    blk_valid = jnp.where(blk_start < pad_end[-1], blk_valid, 0).astype(I32)
    return pos.astype(I32), blk_e, blk_valid, n_rows


def _ple_kernel(x_ref, y_ref, rw_ref, p_ref, g_ref, wg_ref, wp_ref, o_ref, x2_ref, hp_ref, *, tn):
    j = pl.program_id(1)

    @pl.when(j == 0)
    def _():
        rw = rw_ref[...]
        lo0, hi0 = _unpack_bf16_pairs(y_ref[0])
        lo1, hi1 = _unpack_bf16_pairs(y_ref[1])
        w0 = rw[:, 0:1]
        w1 = rw[:, 1:2]
        y = jnp.concatenate([lo0 * w0 + lo1 * w1, hi0 * w0 + hi1 * w1], axis=1)
        x2 = x_ref[...] + y
        for jj in range(x2_ref.shape[0]):
            x2_ref[jj] = x2[:, jj * tn:(jj + 1) * tn]
        hp_ref[...] = _rms(x2, g_ref[...]).astype(BF16)

    gate = jax.nn.sigmoid(jnp.dot(hp_ref[...], wg_ref[...], preferred_element_type=F32))
    proj = jnp.dot(p_ref[...].astype(BF16), wp_ref[...], preferred_element_type=F32)
    o_ref[...] = x2_ref[j] + gate * proj


def moe_combine_ple(x, ysel, rw, p, gain, w_gate, w_proj, tm=512, tn=512):
    T, D = x.shape
    P = p.shape[1]
    tm, tn = _tile(T, tm), _tile(D, tn)
    return pl.pallas_call(
        functools.partial(_ple_kernel, tn=tn),
        grid=(T // tm, D // tn),
        in_specs=[pl.BlockSpec((tm, D), lambda i, j: (i, 0)),
                  pl.BlockSpec((2, tm, D // 2), lambda i, j: (0, i, 0)),
                  pl.BlockSpec((tm, LANES), lambda i, j: (i, 0)),
                  pl.BlockSpec((tm, P), lambda i, j: (i, 0)),
                  pl.BlockSpec((1, D), lambda i, j: (0, 0)),
                  pl.BlockSpec((D, tn), lambda i, j: (0, j)),
                  pl.BlockSpec((P, tn), lambda i, j: (0, j))],
        out_specs=pl.BlockSpec((tm, tn), lambda i, j: (i, j)),
        out_shape=jax.ShapeDtypeStruct((T, D), F32),
        scratch_shapes=[pltpu.VMEM((D // tn, tm, tn), F32), pltpu.VMEM((tm, D), BF16)],
        compiler_params=_params(("parallel", "arbitrary")),
        name="moe_combine_ple",
    )(x, ysel, rw, p, gain.reshape(1, D), w_gate, w_proj)


def _final_norm_kernel(x_ref, g_ref, o_ref):
    o_ref[...] = _rms(x_ref[...], g_ref[...])


def final_norm(x, gain, tm=512):
    T, D = x.shape
    tm = _tile(T, tm)
    return pl.pallas_call(
        _final_norm_kernel,
        grid=(T // tm,),
        in_specs=[pl.BlockSpec((tm, D), lambda i: (i, 0)), pl.BlockSpec((1, D), lambda i: (0, 0))],
        out_specs=pl.BlockSpec((tm, D), lambda i: (i, 0)),
        out_shape=jax.ShapeDtypeStruct((T, D), F32),
        compiler_params=_params(("parallel",)),
        name="final_norm",
    )(x, gain.reshape(1, D))


def _moe_layer(x, p, w):
    T, D = x.shape
    hp, ri, rw = moe_router(x, w["norm_ffn"], w["w_router"], w["b_router"])
    pos, blk_e, blk_valid, n_rows = _dispatch_plan(ri[:, :2], MOE_ROW_BLOCK)
    xs = sc_scatter_rows(hp, pos, n_rows)
    ys = moe_experts(xs, blk_e, blk_valid, w["moe_w_gate"], w["moe_w_up"], w["moe_w_down"])
    ysel = sc_gather_rows(ys, pos.T.reshape(-1)).reshape(2, T, D // 2)
    return moe_combine_ple(x, ysel, rw, p, w["norm_ple"], w["ple_w_gate"], w["ple_w_proj"])


def _trunk(x, p, layers, norm_final):
    B, S, D = x.shape
    x = x.reshape(B * S, D)
    for i, w in enumerate(layers):
        kind = w["kind"]
        if kind == 0:
            qkv = norm_matmul(x, w["norm_mix"], w["w_qkv"], w["qkv_colscale"])
            o = na_attention(qkv, w["na_bias"], B, S)
            x = matmul_residual(o, w["w_o"], x)
        elif kind == 1:
            x = pool_mixer(x, w["norm_mix"], w["pool_w"], w["pool_scale"], B, S)
        else:
            z = norm_matmul(x, w["norm_mix"], w["w_in"], w["in_colscale"], tn=w["in_tn"])
            a = z[:, 3 * D:3 * D + 2 * GLA_GATE_RANK]
            o = gla_mixer_core(z, a, w["w_a2"], w["b_a"], w["g_norm"], B, S, D)
            x = matmul_residual(o, w["w_o"], x)
        x = _moe_layer(x, p[i].reshape(B * S, -1), w)
    return final_norm(x, norm_final).reshape(B, S, D)


def _prepare_layers(norm_mix, norm_ffn, norm_ple, na_w_qkv, na_rpb, na_w_o, pool_w, pool_scale,
                    gla_w_in, gla_w_a2, gla_b_a, gla_norm, gla_w_o, moe_w_rg, moe_b_rg, moe_w_re,
                    moe_b_re, moe_w_gate, moe_w_up, moe_w_down, ple_w_proj, ple_w_gate):
    depth, D = norm_mix.shape
    layers = []
    for i in range(depth):
        j, kind = i // 3, i % 3
        w = {"kind": kind, "norm_mix": norm_mix[i], "norm_ffn": norm_ffn[i], "norm_ple": norm_ple[i]}
        if kind == 0:
            w["w_qkv"] = na_w_qkv[j].astype(BF16)
            w["qkv_colscale"] = jnp.concatenate(
                [jnp.full((D,), NA_HEAD_DIM ** -0.5, F32), jnp.ones((2 * D,), F32)])
            w["na_bias"] = _na_bias_table(na_rpb[j])
            w["w_o"] = na_w_o[j].astype(BF16)
        elif kind == 1:
            w["pool_w"] = pool_w[j].astype(BF16)
            w["pool_scale"] = pool_scale[j]
        else:
            n_in = gla_w_in.shape[2]
            n_pad = -(-n_in // (7 * LANES)) * (7 * LANES)
            w["w_in"] = jnp.pad(gla_w_in[j], ((0, 0), (0, n_pad - n_in))).astype(BF16)
            dkh = D // 2 // GLA_HEADS
            w["in_colscale"] = jnp.concatenate(
                [jnp.full((D // 2,), dkh ** -0.5, F32), jnp.ones((n_pad - D // 2,), F32)])
            w["in_tn"] = n_pad // 7
            w["w_a2"] = gla_w_a2[j]
            w["b_a"] = gla_b_a[j]
            w["g_norm"] = gla_norm[j]
            w["w_o"] = gla_w_o[j].astype(BF16)
        wr = jnp.concatenate([moe_w_rg[i], moe_w_re[i]], axis=1)
        br = jnp.concatenate([moe_b_rg[i], moe_b_re[i]])
        w["w_router"] = jnp.pad(wr, ((0, 0), (0, LANES - wr.shape[1])))
        w["b_router"] = jnp.pad(br, (0, LANES - br.shape[0])).reshape(1, LANES)
        w["moe_w_gate"] = moe_w_gate[i].astype(BF16)
        w["moe_w_up"] = moe_w_up[i].astype(BF16)
        w["moe_w_down"] = moe_w_down[i].astype(BF16)
        w["ple_w_gate"] = ple_w_gate[i].astype(BF16)
        w["ple_w_proj"] = ple_w_proj[i].astype(BF16)
        layers.append(w)
    return layers


def kernel(x_prompt, x_sample, p_prompt, p_sample, norm_mix, norm_ffn, norm_ple, norm_final, na_w_qkv, na_rpb, na_w_o, pool_w, pool_scale, gla_w_in, gla_w_a2, gla_b_a, gla_norm, gla_w_o, moe_w_rg, moe_b_rg, moe_w_re, moe_b_re, moe_w_gate, moe_w_up, moe_w_down, ple_w_proj, ple_w_gate):
    layers = _prepare_layers(norm_mix, norm_ffn, norm_ple, na_w_qkv, na_rpb, na_w_o, pool_w, pool_scale,
                             gla_w_in, gla_w_a2, gla_b_a, gla_norm, gla_w_o, moe_w_rg, moe_b_rg,
                             moe_w_re, moe_b_re, moe_w_gate, moe_w_up, moe_w_down, ple_w_proj, ple_w_gate)
    y_prompt = _trunk(x_prompt, p_prompt, layers, norm_final)
    y_sample = _trunk(x_sample, p_sample, layers, norm_final)
    return (y_prompt, y_sample)
```

```python
import functools

import numpy as np
import jax
import jax.numpy as jnp
from jax import lax
from jax.experimental import pallas as pl
from jax.experimental.pallas import tpu as pltpu
from jax.experimental.pallas import tpu_sc as plsc

F32 = jnp.float32
BF16 = jnp.bfloat16
I32 = jnp.int32
U32 = jnp.uint32

NORM_EPS = 1e-6
LOG2E = 1.4426950408889634
GRID_W = 64
NA_HEAD_DIM = 32
NA_WIN_H = 8
NA_WIN_W = 16
NA_HEADS_PER_STEP = 4
NA_ROW_UNROLL = 16
POOL_WINDOWS = (2, 4, 8, 16)
POOL_HALO = 16
GLA_HEADS = 4
GLA_GATE_RANK = 16
GLA_TAU = 16.0
GLA_CHUNK = 64
MOE_GROUPS = 4
MOE_PER_GROUP = 8
MOE_EXPERTS = MOE_GROUPS * MOE_PER_GROUP
MOE_ROW_BLOCK = 512
ROUTER_TILE = 512
LANES = 128
SC_CORES = 2
SC_WORKERS = SC_CORES * 16
SC_CHUNK = 32
VMEM_LIMIT = 56 * 1024 * 1024


def _tile(n, pref):
    t = min(n, pref)
    assert n % t == 0, (n, pref)
    return t


def _params(sem, vmem=VMEM_LIMIT):
    return pltpu.CompilerParams(dimension_semantics=sem, vmem_limit_bytes=vmem)


def _rms(x, g):
    ms = jnp.mean(x * x, axis=-1, keepdims=True)
    return x * lax.rsqrt(ms + NORM_EPS) * g


def _sigmoid(x):
    return 0.5 * jnp.tanh(0.5 * x) + 0.5


def _norm_mm_kernel(x_ref, g_ref, w_ref, cs_ref, o_ref, h_ref):
    @pl.when(pl.program_id(1) == 0)
    def _():
        h_ref[...] = _rms(x_ref[...], g_ref[...]).astype(BF16)

    acc = jnp.dot(h_ref[...], w_ref[...], preferred_element_type=F32)
    o_ref[...] = (acc * cs_ref[...]).astype(o_ref.dtype)


def norm_matmul(x, gain, w, colscale, out_dtype=BF16, tm=1024, tn=512):
    T, D = x.shape
    N = w.shape[1]
    tm, tn = _tile(T, tm), _tile(N, tn)
    return pl.pallas_call(
        _norm_mm_kernel,
        grid=(T // tm, N // tn),
        in_specs=[pl.BlockSpec((tm, D), lambda i, j: (i, 0)),
                  pl.BlockSpec((1, D), lambda i, j: (0, 0)),
                  pl.BlockSpec((D, tn), lambda i, j: (0, j)),
                  pl.BlockSpec((1, tn), lambda i, j: (0, j))],
        out_specs=pl.BlockSpec((tm, tn), lambda i, j: (i, j)),
        out_shape=jax.ShapeDtypeStruct((T, N), out_dtype),
        scratch_shapes=[pltpu.VMEM((tm, D), BF16)],
        compiler_params=_params(("parallel", "arbitrary")),
        name="norm_matmul",
    )(x, gain.reshape(1, D), w, colscale.reshape(1, N))


def _mm_res_kernel(a_ref, w_ref, r_ref, o_ref):
    o_ref[...] = r_ref[...] + jnp.dot(a_ref[...], w_ref[...], preferred_element_type=F32)


def matmul_residual(a, w, res, tm=1024, tn=512):
    T, K = a.shape
    N = w.shape[1]
    tm, tn = _tile(T, tm), _tile(N, tn)
    return pl.pallas_call(
        _mm_res_kernel,
        grid=(T // tm, N // tn),
        in_specs=[pl.BlockSpec((tm, K), lambda i, j: (i, 0)),
                  pl.BlockSpec((K, tn), lambda i, j: (0, j)),
                  pl.BlockSpec((tm, tn), lambda i, j: (i, j))],
        out_specs=pl.BlockSpec((tm, tn), lambda i, j: (i, j)),
        out_shape=jax.ShapeDtypeStruct((T, N), F32),
        compiler_params=_params(("parallel", "arbitrary")),
        name="matmul_residual",
    )(a, w, res)


def _na_bias_table(rpb):
    H = rpb.shape[0]
    q = np.arange(GRID_W)[:, None]
    c = np.arange(GRID_W)[None, :]
    wstart = np.clip(q - NA_WIN_W // 2, 0, GRID_W - NA_WIN_W)
    ok = (c >= wstart) & (c < wstart + NA_WIN_W)
    dcol = np.clip(c - q + NA_WIN_W - 1, 0, 2 * NA_WIN_W - 2)
    dl = np.arange(NA_WIN_H)[:, None]
    k = np.arange(NA_WIN_H)[None, :]
    drow = k - dl + NA_WIN_H - 1
    b = (rpb.astype(F32) * LOG2E)[:, drow][:, :, :, dcol]
    b = jnp.where(jnp.asarray(ok)[None, None, None], b, -jnp.inf)
    b = b.transpose(0, 1, 3, 2, 4)
    b = b.reshape(H // NA_HEADS_PER_STEP, NA_HEADS_PER_STEP, NA_WIN_H, GRID_W, NA_WIN_H * GRID_W)
    b = b.transpose(0, 2, 1, 3, 4)
    return b.reshape(H // NA_HEADS_PER_STEP, NA_WIN_H, NA_HEADS_PER_STEP * GRID_W, NA_WIN_H * GRID_W)


def _na_kernel(q_ref, k_ref, v_ref, b_ref, o_ref, *, rows):
    nkeys = NA_WIN_H * GRID_W
    lane_head = lax.broadcasted_iota(I32, (GRID_W, LANES), 1) // NA_HEAD_DIM
    masks = [lane_head == h for h in range(NA_HEADS_PER_STEP)]

    def body(r, carry):
        rs = jnp.clip(r - NA_WIN_H // 2, 0, rows - NA_WIN_H)
        q = q_ref[pl.ds(pl.multiple_of(r * GRID_W, GRID_W), GRID_W), :]
        kk = k_ref[pl.ds(pl.multiple_of(rs * GRID_W, GRID_W), nkeys), :]
        vv = v_ref[pl.ds(pl.multiple_of(rs * GRID_W, GRID_W), nkeys), :]
        zero = jnp.zeros_like(q)
        qm = jnp.concatenate([jnp.where(m, q, zero) for m in masks], axis=0)
        s = lax.dot_general(qm, kk, (((1,), (1,)), ((), ())), preferred_element_type=F32)
        s = s + b_ref[r - rs]
        m = jnp.max(s, axis=-1, keepdims=True)
        p = jnp.exp2((s - m).astype(BF16))
        v_ext = jnp.concatenate([vv, jnp.ones_like(vv)], axis=1)
        pv = jnp.dot(p, v_ext, preferred_element_type=F32)
        pv = pv[:, :LANES] / pv[:, LANES:]
        o = jnp.zeros((GRID_W, LANES), F32)
        for h in range(NA_HEADS_PER_STEP):
            o = jnp.where(masks[h], pv[h * GRID_W:(h + 1) * GRID_W], o)
        o_ref[pl.ds(pl.multiple_of(r * GRID_W, GRID_W), GRID_W), :] = o.astype(o_ref.dtype)
        return carry

    lax.fori_loop(0, rows, body, 0, unroll=NA_ROW_UNROLL)


def na_attention(qkv, bias, B, S):
    T, D3 = qkv.shape
    D = D3 // 3
    G = D // LANES
    rows = S // GRID_W
    assert rows >= NA_WIN_H and S % GRID_W == 0
    nq = NA_HEADS_PER_STEP * GRID_W
    nkeys = NA_WIN_H * GRID_W
    return pl.pallas_call(
        functools.partial(_na_kernel, rows=rows),
        grid=(G, B),
        in_specs=[pl.BlockSpec((S, LANES), lambda g, b: (b, g)),
                  pl.BlockSpec((S, LANES), lambda g, b: (b, G + g)),
                  pl.BlockSpec((S, LANES), lambda g, b: (b, 2 * G + g)),
                  pl.BlockSpec((None, NA_WIN_H, nq, nkeys), lambda g, b: (g, 0, 0, 0))],
        out_specs=pl.BlockSpec((S, LANES), lambda g, b: (b, g)),
        out_shape=jax.ShapeDtypeStruct((T, D), BF16),
        compiler_params=_params(("parallel", "parallel")),
        name="na_attention",
    )(qkv, qkv, qkv, bias)


def _pool_kernel(xp_ref, xc_ref, xn_ref, g_ref, w_ref, sc_ref, o_ref, h_ref, *, S, tm):
    i = pl.program_id(1)
    nt = pl.num_programs(1)
    g = g_ref[...]
    D = xc_ref.shape[1]
    dg = D // len(POOL_WINDOWS)
    xc = xc_ref[...]
    hc = _rms(xc, g)
    h_ref[pl.ds(0, POOL_HALO), :] = jnp.where(i > 0, _rms(xp_ref[...], g), 0.0)
    h_ref[pl.ds(POOL_HALO, tm), :] = hc
    h_ref[pl.ds(POOL_HALO + tm, POOL_HALO), :] = jnp.where(i < nt - 1, _rms(xn_ref[...], g), 0.0)
    t = i * tm + lax.broadcasted_iota(I32, (tm, 1), 0)
    for gi, win in enumerate(POOL_WINDOWS):
        c0 = gi * dg
        acc = h_ref[pl.ds(POOL_HALO - win // 2, tm), pl.ds(c0, dg)]
        for o in range(-win // 2 + 1, win // 2):
            acc = acc + h_ref[pl.ds(POOL_HALO + o, tm), pl.ds(c0, dg)]
        cnt = jnp.minimum(t + win // 2, S) - jnp.maximum(t - win // 2, 0)
        d = acc / cnt.astype(F32) - hc[:, c0:c0 + dg]
        y = jnp.dot(d.astype(BF16), w_ref[gi], preferred_element_type=F32)
        o_ref[:, pl.ds(c0, dg)] = xc[:, c0:c0 + dg] + y * sc_ref[:, pl.ds(c0, dg)]


def pool_mixer(x, gain, w_pool, scale, B, S, tm=256):
    T, D = x.shape
    tm = _tile(S, tm)
    nt = S // tm
    hb = tm // POOL_HALO
    nhb = S // POOL_HALO
    return pl.pallas_call(
        functools.partial(_pool_kernel, S=S, tm=tm),
        grid=(B, nt),
        in_specs=[pl.BlockSpec((POOL_HALO, D), lambda b, i: (b * nhb + jnp.maximum(i * hb - 1, 0), 0)),
                  pl.BlockSpec((tm, D), lambda b, i: (b * nt + i, 0)),
                  pl.BlockSpec((POOL_HALO, D),
                               lambda b, i: (b * nhb + jnp.minimum((i + 1) * hb, nhb - 1), 0)),
                  pl.BlockSpec((1, D), lambda b, i: (0, 0)),
                  pl.BlockSpec(w_pool.shape, lambda b, i: (0, 0, 0)),
                  pl.BlockSpec((1, D), lambda b, i: (0, 0))],
        out_specs=pl.BlockSpec((tm, D), lambda b, i: (b * nt + i, 0)),
        out_shape=jax.ShapeDtypeStruct((T, D), F32),
        scratch_shapes=[pltpu.VMEM((tm + 2 * POOL_HALO, D), F32)],
        compiler_params=_params(("parallel", "arbitrary")),
        name="pool_mixer",
    )(x, x, x, gain.reshape(1, D), w_pool, scale.reshape(1, D))


def _gla_kernel(*refs, reverse, finalize, nchunks):
    if finalize:
        (q_ref, k_ref, v_ref, a_ref, wa_ref, ba_ref, of_ref, r_ref, gn_ref, o_ref, st_ref) = refs
    else:
        (q_ref, k_ref, v_ref, a_ref, wa_ref, ba_ref, o_ref, st_ref) = refs
    C = GLA_CHUNK
    H = GLA_HEADS
    dk = q_ref.shape[1] // H
    dv = v_ref.shape[1] // H

    @pl.when(pl.program_id(1) == 0)
    def _():
        st_ref[...] = jnp.zeros_like(st_ref)

    ii = lax.broadcasted_iota(I32, (C, C), 0)
    jj = lax.broadcasted_iota(I32, (C, C), 1)
    causal = (ii <= jj) if reverse else (ii >= jj)
    tri = causal.astype(F32)
    a_lo = GLA_GATE_RANK if reverse else 0
    wa = wa_ref[...]
    ba = ba_ref[...]

    def body(ci, carry):
        c = (nchunks - 1 - ci) if reverse else ci
        sl = pl.ds(pl.multiple_of(c * C, C), C)
        a = a_ref[sl, :][:, a_lo:a_lo + GLA_GATE_RANK].astype(F32)
        logit = jnp.dot(a, wa, precision=lax.Precision.HIGHEST, preferred_element_type=F32) + ba
        log_sig = jnp.minimum(logit, 0.0) - jnp.log1p(jnp.exp(-jnp.abs(logit)))
        la = log_sig * (1.0 / GLA_TAU)
        bc_all = jnp.dot(tri, la, precision=lax.Precision.HIGHEST, preferred_element_type=F32)
        bl_all = jnp.sum(la, axis=0, keepdims=True)
        for h in range(H):
            kcol = slice(h * dk, (h + 1) * dk)
            vcol = slice(h * dv, (h + 1) * dv)
            bc = bc_all[:, kcol]
            bl = bl_all[:, kcol]
            q = q_ref[sl, kcol].astype(F32)
            k = k_ref[sl, kcol].astype(F32)
            v = v_ref[sl, vcol]
            q_in = (q * jnp.exp(bc)).astype(BF16)
            k_in = (k * jnp.exp(-bc)).astype(BF16)
            k_end = (k * jnp.exp(bl - bc)).astype(BF16)
            att = lax.dot_general(q_in, k_in, (((1,), (1,)), ((), ())), preferred_element_type=F32)
            att = jnp.where(causal, att, 0.0).astype(BF16)
            st = st_ref[h]
            o = jnp.dot(att, v, preferred_element_type=F32)
            o = o + lax.dot_general(q_in, st.astype(BF16), (((1,), (1,)), ((), ())),
                                    preferred_element_type=F32)
            st_ref[h] = st * jnp.exp(bl) + lax.dot_general(
                v, k_end, (((0,), (0,)), ((), ())), preferred_element_type=F32)
            if finalize:
                o = o + of_ref[sl, vcol].astype(F32)
                o = o * lax.rsqrt(jnp.mean(o * o, axis=-1, keepdims=True) + NORM_EPS) * gn_ref[...]
                r = r_ref[sl, vcol].astype(F32)
                o = o * (r * _sigmoid(r))
            o_ref[sl, vcol] = o.astype(o_ref.dtype)
        return carry

    lax.fori_loop(0, nchunks, body, 0)


def _gla_pass(z, a, w_a2, b_a, B, S, D, *, reverse, o_fwd=None, g_norm=None, tb=512):
    T = z.shape[0]
    H = GLA_HEADS
    DK = D // 2
    tb = _tile(S, tb)
    nb = S // tb
    finalize = o_fwd is not None
    d = 1 if reverse else 0

    def rb(b, i):
        return b * nb + ((nb - 1 - i) if reverse else i)

    in_specs = [pl.BlockSpec((tb, DK), lambda b, i: (rb(b, i), 0)),
                pl.BlockSpec((tb, DK), lambda b, i: (rb(b, i), 1)),
                pl.BlockSpec((tb, D), lambda b, i: (rb(b, i), 1)),
                pl.BlockSpec((tb, 2 * GLA_GATE_RANK), lambda b, i: (rb(b, i), 0)),
                pl.BlockSpec((None, GLA_GATE_RANK, DK), lambda b, i: (d, 0, 0)),
                pl.BlockSpec((None, 1, DK), lambda b, i: (d, 0, 0))]
    args = [z, z, z, a, w_a2, b_a.reshape(2, 1, DK)]
    if finalize:
        in_specs += [pl.BlockSpec((tb, D), lambda b, i: (rb(b, i), 0)),
                     pl.BlockSpec((tb, D), lambda b, i: (rb(b, i), 2)),
                     pl.BlockSpec((1, D // H), lambda b, i: (0, 0))]
        args += [o_fwd, z, g_norm.reshape(1, D // H)]
    return pl.pallas_call(
        functools.partial(_gla_kernel, reverse=reverse, finalize=finalize, nchunks=tb // GLA_CHUNK),
        grid=(B, nb),
        in_specs=in_specs,
        out_specs=pl.BlockSpec((tb, D), lambda b, i: (rb(b, i), 0)),
        out_shape=jax.ShapeDtypeStruct((T, D), BF16),
        scratch_shapes=[pltpu.VMEM((H, D // H, DK // H), F32)],
        compiler_params=_params(("parallel", "arbitrary")),
        name="gla_bwd_finalize" if finalize else "gla_fwd",
    )(*args)


def gla_mixer_core(z, a, w_a2, b_a, g_norm, B, S, D):
    o_f = _gla_pass(z, a, w_a2, b_a, B, S, D, reverse=False)
    return _gla_pass(z, a, w_a2, b_a, B, S, D, reverse=True, o_fwd=o_f, g_norm=g_norm)


def _pack_bf16_pairs(x):
    m = x.shape[1] // 2
    xb = x.astype(BF16).astype(F32)
    lo = pltpu.bitcast(xb[:, :m], U32) >> 16
    hi = pltpu.bitcast(xb[:, m:], U32)
    return hi | lo


def _unpack_bf16_pairs(p):
    lo = pltpu.bitcast(p << 16, F32)
    hi = pltpu.bitcast(p & jnp.uint32(0xFFFF0000), F32)
    return lo, hi


def _router_kernel(x_ref, g_ref, w_ref, b_ref, hp_ref, ri_ref, rw_ref, cnt_ref):
    h = _rms(x_ref[...], g_ref[...])
    hp_ref[...] = _pack_bf16_pairs(h)
    logits = jnp.dot(h, w_ref[...], precision=lax.Precision.HIGHEST,
                     preferred_element_type=F32) + b_ref[...]
    lane = lax.broadcasted_iota(I32, logits.shape, 1)
    lane_f = lane.astype(F32)
    neg = -jnp.inf

    def first_lane(hit):
        return jnp.min(jnp.where(hit, lane_f, float(LANES)), axis=-1, keepdims=True).astype(I32)

    gl = jnp.where(lane < MOE_GROUPS, logits, neg)
    gm = jnp.max(gl, axis=-1, keepdims=True)
    g_top = 1.0 / jnp.sum(jnp.exp(gl - gm), axis=-1, keepdims=True)
    g_idx = first_lane(gl == gm)
    e_lane = lane - MOE_GROUPS
    in_grp = (e_lane >= 0) & (e_lane < MOE_EXPERTS) & ((e_lane >> 3) == g_idx)
    el = jnp.where(in_grp, logits, neg)
    m1 = jnp.max(el, axis=-1, keepdims=True)
    es = jnp.sum(jnp.exp(el - m1), axis=-1, keepdims=True)
    i1 = first_lane(el == m1)
    el2 = jnp.where(lane == i1, neg, el)
    m2 = jnp.max(el2, axis=-1, keepdims=True)
    i2 = first_lane(el2 == m2)
    p1 = 1.0 / es
    p2 = jnp.exp(m2 - m1) / es
    w1 = g_top * p1 / (p1 + p2)
    w2 = g_top * p2 / (p1 + p2)
    e1 = i1 - MOE_GROUPS
    e2 = i2 - MOE_GROUPS
    tm = logits.shape[0]
    onehot = jnp.where((lane == e1) | (lane == e2), 1.0, 0.0)
    before = (lax.broadcasted_iota(I32, (tm, tm), 0) > lax.broadcasted_iota(I32, (tm, tm), 1))
    rank = jnp.dot(before.astype(BF16), onehot.astype(BF16), preferred_element_type=F32)
    r1 = jnp.sum(jnp.where(lane == e1, rank, 0.0), axis=-1, keepdims=True).astype(I32)
    r2 = jnp.sum(jnp.where(lane == e2, rank, 0.0), axis=-1, keepdims=True).astype(I32)
    ri_ref[...] = jnp.where(lane == 0, e1, jnp.where(lane == 1, e2,
                            jnp.where(lane == 2, r1, jnp.where(lane == 3, r2, 0))))
    rw_ref[...] = jnp.where(lane == 0, w1, jnp.where(lane == 1, w2, 0.0))
    cnt = jnp.sum(onehot, axis=0, keepdims=True).astype(I32)
    cnt_ref[...] = jnp.broadcast_to(cnt, cnt_ref.shape)


def moe_router(x, gain, w_router, b_router, tm=ROUTER_TILE):
    T, D = x.shape
    tm = _tile(T, tm)
    return pl.pallas_call(
        _router_kernel,
        grid=(T // tm,),
        in_specs=[pl.BlockSpec((tm, D), lambda i: (i, 0)),
                  pl.BlockSpec((1, D), lambda i: (0, 0)),
                  pl.BlockSpec((D, LANES), lambda i: (0, 0)),
                  pl.BlockSpec((1, LANES), lambda i: (0, 0))],
        out_specs=[pl.BlockSpec((tm, D // 2), lambda i: (i, 0)),
                   pl.BlockSpec((tm, LANES), lambda i: (i, 0)),
                   pl.BlockSpec((tm, LANES), lambda i: (i, 0)),
                   pl.BlockSpec((None, 8, LANES), lambda i: (i, 0, 0))],
        out_shape=[jax.ShapeDtypeStruct((T, D // 2), U32),
                   jax.ShapeDtypeStruct((T, LANES), I32),
                   jax.ShapeDtypeStruct((T, LANES), F32),
                   jax.ShapeDtypeStruct((T // tm, 8, LANES), I32)],
        compiler_params=_params(("parallel",)),
        name="moe_router",
    )(x, gain.reshape(1, D), w_router, b_router)


def _expert_kernel(be_ref, bv_ref, x_ref, wg_ref, wu_ref, wd_ref, o_ref):
    i = pl.program_id(0)
    nvalid = bv_ref[i]

    @pl.when(nvalid > 0)
    def _():
        row = lax.broadcasted_iota(I32, x_ref.shape, 0)
        xp = jnp.where(row < nvalid, x_ref[...], jnp.uint32(0))
        lo, hi = _unpack_bf16_pairs(xp)
        x = jnp.concatenate([lo.astype(BF16), hi.astype(BF16)], axis=1)
        a = jnp.dot(x, wg_ref[...], preferred_element_type=F32)
        u = jnp.dot(x, wu_ref[...], preferred_element_type=F32)
        hmid = (a * _sigmoid(a) * u).astype(BF16)
        y = jnp.dot(hmid, wd_ref[...], preferred_element_type=F32)
        o_ref[...] = _pack_bf16_pairs(y)

    @pl.when(nvalid == 0)
    def _():
        o_ref[...] = jnp.zeros_like(o_ref)


def moe_experts(xs, blk_e, blk_valid, w_gate, w_up, w_down, layer):
    n_rows, dh = xs.shape
    _, E, D, FF = w_gate.shape
    bm = MOE_ROW_BLOCK
    nblk = n_rows // bm
    grid_spec = pltpu.PrefetchScalarGridSpec(
        num_scalar_prefetch=2,
        grid=(nblk,),
        in_specs=[pl.BlockSpec((bm, dh), lambda i, be, bv: (i, 0)),
                  pl.BlockSpec((None, None, D, FF), lambda i, be, bv: (layer, be[i], 0, 0)),
                  pl.BlockSpec((None, None, D, FF), lambda i, be, bv: (layer, be[i], 0, 0)),
                  pl.BlockSpec((None, None, FF, D), lambda i, be, bv: (layer, be[i], 0, 0))],
        out_specs=pl.BlockSpec((bm, dh), lambda i, be, bv: (i, 0)),
    )
    return pl.pallas_call(
        _expert_kernel,
        grid_spec=grid_spec,
        out_shape=jax.ShapeDtypeStruct((n_rows, dh), U32),
        compiler_params=_params(("arbitrary",)),
        name="moe_experts",
    )(blk_e, blk_valid, xs, w_gate, w_up, w_down)


def _sc_mesh():
    return plsc.VectorSubcoreMesh(core_axis_name="c", subcore_axis_name="s")


def _sc_worker_id():
    return lax.axis_index("s") * SC_CORES + lax.axis_index("c")


def sc_scatter_rows(src, pos, n_rows):
    T, W = src.shape
    per_w = T // SC_WORKERS
    nch = per_w // SC_CHUNK
    assert per_w * SC_WORKERS == T and nch * SC_CHUNK == per_w
    idx = pos.reshape(SC_WORKERS, nch, SC_CHUNK, 2).transpose(0, 1, 3, 2).reshape(SC_WORKERS, 2 * nch, SC_CHUNK)

    @functools.partial(
        pl.kernel, mesh=_sc_mesh(),
        out_type=jax.ShapeDtypeStruct((n_rows, W), src.dtype),
        scratch_types=[pltpu.VMEM((2 * nch, SC_CHUNK), I32),
                       pltpu.VMEM((SC_CHUNK, W), src.dtype),
                       pltpu.SemaphoreType.DMA],
    )
    def k(src_hbm, idx_hbm, out_hbm, idx_v, rows_v, sem):
        wid = _sc_worker_id()
        base = wid * per_w
        pltpu.sync_copy(idx_hbm.at[wid], idx_v)

        @pl.loop(0, nch)
        def _(c):
            pltpu.sync_copy(src_hbm.at[pl.ds(base + c * SC_CHUNK, SC_CHUNK)], rows_v)
            pltpu.async_copy(rows_v, out_hbm.at[idx_v.at[2 * c]], sem).wait()
            pltpu.async_copy(rows_v, out_hbm.at[idx_v.at[2 * c + 1]], sem).wait()

    return k(src, idx)


def sc_gather_rows(table, idx_flat):
    N = idx_flat.shape[0]
    W = table.shape[1]
    per_w = N // SC_WORKERS
    nch = per_w // SC_CHUNK
    assert per_w * SC_WORKERS == N and nch * SC_CHUNK == per_w
    idx = idx_flat.reshape(SC_WORKERS, nch, SC_CHUNK)

    @functools.partial(
        pl.kernel, mesh=_sc_mesh(),
        out_type=jax.ShapeDtypeStruct((N, W), table.dtype),
        scratch_types=[pltpu.VMEM((nch, SC_CHUNK), I32),
                       pltpu.VMEM((SC_CHUNK, W), table.dtype),
                       pltpu.SemaphoreType.DMA],
    )
    def k(table_hbm, idx_hbm, out_hbm, idx_v, rows_v, sem):
        wid = _sc_worker_id()
        base = wid * per_w
        pltpu.sync_copy(idx_hbm.at[wid], idx_v)

        @pl.loop(0, nch)
        def _(c):
            pltpu.async_copy(table_hbm.at[idx_v.at[c]], rows_v, sem).wait()
            pltpu.sync_copy(rows_v, out_hbm.at[pl.ds(base + c * SC_CHUNK, SC_CHUNK)])

    return k(table, idx)


def _dispatch_plan(ri, tile_counts, bm):
    T = ri.shape[0]
    E = MOE_EXPERTS
    ntiles = tile_counts.shape[0]
    n_rows = (T * 2 // bm + E) * bm
    nblk = n_rows // bm
    eidx = ri[:, 0:2]
    tile_end = jnp.cumsum(tile_counts, axis=0)
    counts = tile_end[-1]
    padded = (counts + bm - 1) // bm * bm
    pad_end = jnp.cumsum(padded)
    base = pad_end - padded
    start = (base[None, :] + tile_end - tile_counts).reshape(-1)
    tile_of = jnp.arange(T, dtype=I32)[:, None] // (T // ntiles)
    pos = start[tile_of * E + eidx] + ri[:, 2:4]
    blk_start = jnp.arange(nblk, dtype=I32) * bm
    blk_e = jnp.minimum(jnp.searchsorted(pad_end, blk_start, side="right"), E - 1).astype(I32)
    blk_valid = jnp.clip(counts[blk_e] - (blk_start - base[blk_e]), 0, bm)
    blk_valid = jnp.where(blk_start < pad_end[-1], blk_valid, 0).astype(I32)
    return pos.astype(I32), blk_e, blk_valid, n_rows


def _ple_kernel(x_ref, y_ref, rw_ref, p_ref, g_ref, wg_ref, wp_ref, o_ref, *, sub, tn):
    tm, D = x_ref.shape
    g = g_ref[...]
    for s0 in range(0, tm, sub):
        rows = pl.ds(s0, sub)
        rw = rw_ref[rows, :]
        lo0, hi0 = _unpack_bf16_pairs(y_ref[0, rows, :])
        lo1, hi1 = _unpack_bf16_pairs(y_ref[1, rows, :])
        w0 = rw[:, 0:1]
        w1 = rw[:, 1:2]
        y = jnp.concatenate([lo0 * w0 + lo1 * w1, hi0 * w0 + hi1 * w1], axis=1)
        x2 = x_ref[rows, :] + y
        hp = _rms(x2, g).astype(BF16)
        pb = p_ref[rows, :].astype(BF16)
        for c0 in range(0, D, tn):
            gate = _sigmoid(jnp.dot(hp, wg_ref[:, c0:c0 + tn], preferred_element_type=F32))
            proj = jnp.dot(pb, wp_ref[:, c0:c0 + tn], preferred_element_type=F32)
            o_ref[rows, c0:c0 + tn] = x2[:, c0:c0 + tn] + gate * proj


def moe_combine_ple(x, ysel, rw, p, gain, w_gate, w_proj, layer, tm=256, sub=128, tn=512):
    T, D = x.shape
    P = p.shape[2]
    tm, tn = _tile(T, tm), _tile(D, tn)
    sub = _tile(tm, sub)
    return pl.pallas_call(
        functools.partial(_ple_kernel, sub=sub, tn=tn),
        grid=(T // tm,),
        in_specs=[pl.BlockSpec((tm, D), lambda i: (i, 0)),
                  pl.BlockSpec((2, tm, D // 2), lambda i: (0, i, 0)),
                  pl.BlockSpec((tm, LANES), lambda i: (i, 0)),
                  pl.BlockSpec((None, tm, P), lambda i: (layer, i, 0)),
                  pl.BlockSpec((1, D), lambda i: (0, 0)),
                  pl.BlockSpec((None, D, D), lambda i: (layer, 0, 0)),
                  pl.BlockSpec((None, P, D), lambda i: (layer, 0, 0))],
        out_specs=pl.BlockSpec((tm, D), lambda i: (i, 0)),
        out_shape=jax.ShapeDtypeStruct((T, D), F32),
        compiler_params=_params(("parallel",)),
        name="moe_combine_ple",
    )(x, ysel, rw, p, gain.reshape(1, D), w_gate, w_proj)


def _final_norm_kernel(x_ref, g_ref, o_ref):
    o_ref[...] = _rms(x_ref[...], g_ref[...])


def final_norm(x, gain, tm=512):
    T, D = x.shape
    tm = _tile(T, tm)
    return pl.pallas_call(
        _final_norm_kernel,
        grid=(T // tm,),
        in_specs=[pl.BlockSpec((tm, D), lambda i: (i, 0)), pl.BlockSpec((1, D), lambda i: (0, 0))],
        out_specs=pl.BlockSpec((tm, D), lambda i: (i, 0)),
        out_shape=jax.ShapeDtypeStruct((T, D), F32),
        compiler_params=_params(("parallel",)),
        name="final_norm",
    )(x, gain.reshape(1, D))


def _moe_layer(x, p, w, shared, layer):
    T, D = x.shape
    hp, ri, rw, cnt = moe_router(x, w["norm_ffn"], w["w_router"], w["b_router"])
    pos, blk_e, blk_valid, n_rows = _dispatch_plan(ri, cnt[:, 0, :MOE_EXPERTS], MOE_ROW_BLOCK)
    xs = sc_scatter_rows(hp, pos, n_rows)
    ys = moe_experts(xs, blk_e, blk_valid, shared["moe_w_gate"], shared["moe_w_up"],
                     shared["moe_w_down"], layer)
    ysel = sc_gather_rows(ys, pos.T.reshape(-1)).reshape(2, T, D // 2)
    return moe_combine_ple(x, ysel, rw, p, w["norm_ple"], shared["ple_w_gate"],
                           shared["ple_w_proj"], layer)


def _trunk(x, p, layers, shared, norm_final):
    B, S, D = x.shape
    x = x.reshape(B * S, D)
    p = p.reshape(p.shape[0], B * S, p.shape[-1])
    for i, w in enumerate(layers):
        kind = w["kind"]
        if kind == 0:
            qkv = norm_matmul(x, w["norm_mix"], w["w_qkv"], w["qkv_colscale"])
            o = na_attention(qkv, w["na_bias"], B, S)
            x = matmul_residual(o, w["w_o"], x)
        elif kind == 1:
            x = pool_mixer(x, w["norm_mix"], w["pool_w"], w["pool_scale"], B, S)
        else:
            z = norm_matmul(x, w["norm_mix"], w["w_in"], w["in_colscale"], tn=w["in_tn"])
            a = z[:, 3 * D:3 * D + 2 * GLA_GATE_RANK]
            o = gla_mixer_core(z, a, w["w_a2"], w["b_a"], w["g_norm"], B, S, D)
            x = matmul_residual(o, w["w_o"], x)
        x = _moe_layer(x, p, w, shared, i)
    return final_norm(x, norm_final).reshape(B, S, D)


def _prepare_layers(norm_mix, norm_ffn, norm_ple, na_w_qkv, na_rpb, na_w_o, pool_w, pool_scale,
                    gla_w_in, gla_w_a2, gla_b_a, gla_norm, gla_w_o, moe_w_rg, moe_b_rg, moe_w_re,
                    moe_b_re, moe_w_gate, moe_w_up, moe_w_down, ple_w_proj, ple_w_gate):
    depth, D = norm_mix.shape
    layers = []
    for i in range(depth):
        j, kind = i // 3, i % 3
        w = {"kind": kind, "norm_mix": norm_mix[i], "norm_ffn": norm_ffn[i], "norm_ple": norm_ple[i]}
        if kind == 0:
            w["w_qkv"] = na_w_qkv[j].astype(BF16)
            w["qkv_colscale"] = jnp.concatenate(
                [jnp.full((D,), NA_HEAD_DIM ** -0.5 * LOG2E, F32), jnp.ones((2 * D,), F32)])
            w["na_bias"] = _na_bias_table(na_rpb[j])
            w["w_o"] = na_w_o[j].astype(BF16)
        elif kind == 1:
            w["pool_w"] = pool_w[j].astype(BF16)
            w["pool_scale"] = pool_scale[j]
        else:
            n_in = gla_w_in.shape[2]
            n_pad = -(-n_in // (7 * LANES)) * (7 * LANES)
            w["w_in"] = jnp.pad(gla_w_in[j], ((0, 0), (0, n_pad - n_in))).astype(BF16)
            dkh = D // 2 // GLA_HEADS
            w["in_colscale"] = jnp.concatenate(
                [jnp.full((D // 2,), dkh ** -0.5, F32), jnp.ones((n_pad - D // 2,), F32)])
            w["in_tn"] = n_pad // 7
            w["w_a2"] = gla_w_a2[j]
            w["b_a"] = gla_b_a[j]
            w["g_norm"] = gla_norm[j]
            w["w_o"] = gla_w_o[j].astype(BF16)
        wr = jnp.concatenate([moe_w_rg[i], moe_w_re[i]], axis=1)
        br = jnp.concatenate([moe_b_rg[i], moe_b_re[i]])
        w["w_router"] = jnp.pad(wr, ((0, 0), (0, LANES - wr.shape[1])))
        w["b_router"] = jnp.pad(br, (0, LANES - br.shape[0])).reshape(1, LANES)
        layers.append(w)
    shared = {"moe_w_gate": moe_w_gate.astype(BF16), "moe_w_up": moe_w_up.astype(BF16),
              "moe_w_down": moe_w_down.astype(BF16), "ple_w_gate": ple_w_gate.astype(BF16),
              "ple_w_proj": ple_w_proj.astype(BF16)}
    return layers, shared


def kernel(x_prompt, x_sample, p_prompt, p_sample, norm_mix, norm_ffn, norm_ple, norm_final, na_w_qkv, na_rpb, na_w_o, pool_w, pool_scale, gla_w_in, gla_w_a2, gla_b_a, gla_norm, gla_w_o, moe_w_rg, moe_b_rg, moe_w_re, moe_b_re, moe_w_gate, moe_w_up, moe_w_down, ple_w_proj, ple_w_gate):
    layers, shared = _prepare_layers(
        norm_mix, norm_ffn, norm_ple, na_w_qkv, na_rpb, na_w_o, pool_w, pool_scale, gla_w_in, gla_w_a2,
        gla_b_a, gla_norm, gla_w_o, moe_w_rg, moe_b_rg, moe_w_re, moe_b_re, moe_w_gate, moe_w_up,
        moe_w_down, ple_w_proj, ple_w_gate)
    y_prompt = _trunk(x_prompt, p_prompt, layers, shared, norm_final)
    y_sample = _trunk(x_sample, p_sample, layers, shared, norm_final)
    return (y_prompt, y_sample)
```

```python
import functools

import numpy as np
import jax
import jax.numpy as jnp
from jax import lax
from jax.experimental import pallas as pl
from jax.experimental.pallas import tpu as pltpu
from jax.experimental.pallas import tpu_sc as plsc

F32 = jnp.float32
BF16 = jnp.bfloat16
I32 = jnp.int32
U32 = jnp.uint32

NORM_EPS = 1e-6
LOG2E = 1.4426950408889634
GRID_W = 64
NA_HEAD_DIM = 32
NA_WIN_H = 8
NA_WIN_W = 16
NA_HEADS_PER_STEP = 4
NA_ROW_UNROLL = 16
POOL_WINDOWS = (2, 4, 8, 16)
POOL_HALO = 16
GLA_HEADS = 4
GLA_GATE_RANK = 16
GLA_TAU = 16.0
GLA_CHUNK = 64
MOE_GROUPS = 4
MOE_PER_GROUP = 8
MOE_EXPERTS = MOE_GROUPS * MOE_PER_GROUP
MOE_ROW_BLOCK = 512
ROUTER_TILE = 512
LANES = 128
SC_CORES = 2
SC_WORKERS = SC_CORES * 16
SC_CHUNK = 32
VMEM_LIMIT = 56 * 1024 * 1024


def _tile(n, pref):
    t = min(n, pref)
    assert n % t == 0, (n, pref)
    return t


def _params(sem, vmem=VMEM_LIMIT):
    return pltpu.CompilerParams(dimension_semantics=sem, vmem_limit_bytes=vmem)


def _rms(x, g):
    ms = jnp.mean(x * x, axis=-1, keepdims=True)
    return x * lax.rsqrt(ms + NORM_EPS) * g


def _sigmoid(x):
    return 0.5 * jnp.tanh(0.5 * x) + 0.5


def _norm_mm_kernel(x_ref, g_ref, w_ref, cs_ref, o_ref, h_ref):
    @pl.when(pl.program_id(1) == 0)
    def _():
        h_ref[...] = _rms(x_ref[...], g_ref[...]).astype(BF16)

    acc = jnp.dot(h_ref[...], w_ref[...], preferred_element_type=F32)
    o_ref[...] = (acc * cs_ref[...]).astype(o_ref.dtype)


def norm_matmul(x, gain, w, colscale, out_dtype=BF16, tm=1024, tn=1024):
    T, D = x.shape
    N = w.shape[1]
    tm, tn = _tile(T, tm), _tile(N, tn)
    return pl.pallas_call(
        _norm_mm_kernel,
        grid=(T // tm, N // tn),
        in_specs=[pl.BlockSpec((tm, D), lambda i, j: (i, 0)),
                  pl.BlockSpec((1, D), lambda i, j: (0, 0)),
                  pl.BlockSpec((D, tn), lambda i, j: (0, j)),
                  pl.BlockSpec((1, tn), lambda i, j: (0, j))],
        out_specs=pl.BlockSpec((tm, tn), lambda i, j: (i, j)),
        out_shape=jax.ShapeDtypeStruct((T, N), out_dtype),
        scratch_shapes=[pltpu.VMEM((tm, D), BF16)],
        compiler_params=_params(("parallel", "arbitrary")),
        name="norm_matmul",
    )(x, gain.reshape(1, D), w, colscale.reshape(1, N))


def _mm_res_kernel(a_ref, w_ref, r_ref, o_ref):
    o_ref[...] = r_ref[...] + jnp.dot(a_ref[...], w_ref[...], preferred_element_type=F32)


def matmul_residual(a, w, res, tm=1024, tn=1024):
    T, K = a.shape
    N = w.shape[1]
    tm, tn = _tile(T, tm), _tile(N, tn)
    return pl.pallas_call(
        _mm_res_kernel,
        grid=(T // tm, N // tn),
        in_specs=[pl.BlockSpec((tm, K), lambda i, j: (i, 0)),
                  pl.BlockSpec((K, tn), lambda i, j: (0, j)),
                  pl.BlockSpec((tm, tn), lambda i, j: (i, j))],
        out_specs=pl.BlockSpec((tm, tn), lambda i, j: (i, j)),
        out_shape=jax.ShapeDtypeStruct((T, N), F32),
        compiler_params=_params(("parallel", "arbitrary")),
        name="matmul_residual",
    )(a, w, res)


def _na_bias_table(rpb):
    H, nrow, ncol = rpb.shape
    G = H // NA_HEADS_PER_STEP
    per_head = nrow * ncol
    masked = NA_HEADS_PER_STEP * per_head
    dl, h, q, k, c = np.meshgrid(np.arange(NA_WIN_H), np.arange(NA_HEADS_PER_STEP), np.arange(GRID_W),
                                 np.arange(NA_WIN_H), np.arange(GRID_W), indexing="ij")
    wstart = np.clip(q - NA_WIN_W // 2, 0, GRID_W - NA_WIN_W)
    ok = (c >= wstart) & (c < wstart + NA_WIN_W)
    dcol = np.clip(c - q + NA_WIN_W - 1, 0, 2 * NA_WIN_W - 2)
    drow = k - dl + NA_WIN_H - 1
    idx = np.where(ok, h * per_head + drow * ncol + dcol, masked).astype(np.int32)
    idx = idx.reshape(NA_WIN_H, NA_HEADS_PER_STEP * GRID_W, NA_WIN_H * GRID_W)
    flat = (rpb.astype(F32) * LOG2E).reshape(G, masked)
    flat = jnp.concatenate([flat, jnp.full((G, 1), -jnp.inf, F32)], axis=1)
    return jnp.take(flat, jnp.asarray(idx), axis=1)


def _na_kernel(q_ref, k_ref, v_ref, b_ref, o_ref, *, rows):
    nkeys = NA_WIN_H * GRID_W
    lane_head = lax.broadcasted_iota(I32, (GRID_W, LANES), 1) // NA_HEAD_DIM
    masks = [lane_head == h for h in range(NA_HEADS_PER_STEP)]

    def body(r, carry):
        rs = jnp.clip(r - NA_WIN_H // 2, 0, rows - NA_WIN_H)
        q = q_ref[pl.ds(pl.multiple_of(r * GRID_W, GRID_W), GRID_W), :]
        kk = k_ref[pl.ds(pl.multiple_of(rs * GRID_W, GRID_W), nkeys), :]
        vv = v_ref[pl.ds(pl.multiple_of(rs * GRID_W, GRID_W), nkeys), :]
        zero = jnp.zeros_like(q)
        qm = jnp.concatenate([jnp.where(m, q, zero) for m in masks], axis=0)
        s = lax.dot_general(qm, kk, (((1,), (1,)), ((), ())), preferred_element_type=F32)
        s = s + b_ref[r - rs]
        m = jnp.max(s, axis=-1, keepdims=True)
        p = jnp.exp2((s - m).astype(BF16))
        v_ext = jnp.concatenate([vv, jnp.ones_like(vv)], axis=1)
        pv = jnp.dot(p, v_ext, preferred_element_type=F32)
        pv = pv[:, :LANES] / pv[:, LANES:]
        o = jnp.zeros((GRID_W, LANES), F32)
        for h in range(NA_HEADS_PER_STEP):
            o = jnp.where(masks[h], pv[h * GRID_W:(h + 1) * GRID_W], o)
        o_ref[pl.ds(pl.multiple_of(r * GRID_W, GRID_W), GRID_W), :] = o.astype(o_ref.dtype)
        return carry

    lax.fori_loop(0, rows, body, 0, unroll=NA_ROW_UNROLL)


def na_attention(qkv, bias, B, S):
    T, D3 = qkv.shape
    D = D3 // 3
    G = D // LANES
    rows = S // GRID_W
    assert rows >= NA_WIN_H and S % GRID_W == 0
    nq = NA_HEADS_PER_STEP * GRID_W
    nkeys = NA_WIN_H * GRID_W
    return pl.pallas_call(
        functools.partial(_na_kernel, rows=rows),
        grid=(G, B),
        in_specs=[pl.BlockSpec((S, LANES), lambda g, b: (b, g)),
                  pl.BlockSpec((S, LANES), lambda g, b: (b, G + g)),
                  pl.BlockSpec((S, LANES), lambda g, b: (b, 2 * G + g)),
                  pl.BlockSpec((None, NA_WIN_H, nq, nkeys), lambda g, b: (g, 0, 0, 0))],
        out_specs=pl.BlockSpec((S, LANES), lambda g, b: (b, g)),
        out_shape=jax.ShapeDtypeStruct((T, D), BF16),
        compiler_params=_params(("parallel", "parallel")),
        name="na_attention",
    )(qkv, qkv, qkv, bias)


def _pool_kernel(xp_ref, xc_ref, xn_ref, g_ref, w_ref, sc_ref, o_ref, h_ref, *, S, tm):
    i = pl.program_id(1)
    nt = pl.num_programs(1)
    g = g_ref[...]
    D = xc_ref.shape[1]
    dg = D // len(POOL_WINDOWS)
    xc = xc_ref[...]
    hc = _rms(xc, g)
    h_ref[pl.ds(0, POOL_HALO), :] = jnp.where(i > 0, _rms(xp_ref[...], g), 0.0)
    h_ref[pl.ds(POOL_HALO, tm), :] = hc
    h_ref[pl.ds(POOL_HALO + tm, POOL_HALO), :] = jnp.where(i < nt - 1, _rms(xn_ref[...], g), 0.0)
    t = i * tm + lax.broadcasted_iota(I32, (tm, 1), 0)
    for gi, win in enumerate(POOL_WINDOWS):
        c0 = gi * dg
        acc = h_ref[pl.ds(POOL_HALO - win // 2, tm), pl.ds(c0, dg)]
        for o in range(-win // 2 + 1, win // 2):
            acc = acc + h_ref[pl.ds(POOL_HALO + o, tm), pl.ds(c0, dg)]
        cnt = jnp.minimum(t + win // 2, S) - jnp.maximum(t - win // 2, 0)
        d = acc / cnt.astype(F32) - hc[:, c0:c0 + dg]
        y = jnp.dot(d.astype(BF16), w_ref[gi], preferred_element_type=F32)
        o_ref[:, pl.ds(c0, dg)] = xc[:, c0:c0 + dg] + y * sc_ref[:, pl.ds(c0, dg)]


def pool_mixer(x, gain, w_pool, scale, B, S, tm=256):
    T, D = x.shape
    tm = _tile(S, tm)
    nt = S // tm
    hb = tm // POOL_HALO
    nhb = S // POOL_HALO
    return pl.pallas_call(
        functools.partial(_pool_kernel, S=S, tm=tm),
        grid=(B, nt),
        in_specs=[pl.BlockSpec((POOL_HALO, D), lambda b, i: (b * nhb + jnp.maximum(i * hb - 1, 0), 0)),
                  pl.BlockSpec((tm, D), lambda b, i: (b * nt + i, 0)),
                  pl.BlockSpec((POOL_HALO, D),
                               lambda b, i: (b * nhb + jnp.minimum((i + 1) * hb, nhb - 1), 0)),
                  pl.BlockSpec((1, D), lambda b, i: (0, 0)),
                  pl.BlockSpec(w_pool.shape, lambda b, i: (0, 0, 0)),
                  pl.BlockSpec((1, D), lambda b, i: (0, 0))],
        out_specs=pl.BlockSpec((tm, D), lambda b, i: (b * nt + i, 0)),
        out_shape=jax.ShapeDtypeStruct((T, D), F32),
        scratch_shapes=[pltpu.VMEM((tm + 2 * POOL_HALO, D), F32)],
        compiler_params=_params(("parallel", "arbitrary")),
        name="pool_mixer",
    )(x, x, x, gain.reshape(1, D), w_pool, scale.reshape(1, D))


def _gla_kernel(*refs, reverse, finalize, nchunks):
    if finalize:
        (q_ref, k_ref, v_ref, a_ref, wa_ref, ba_ref, of_ref, r_ref, gn_ref, o_ref, st_ref) = refs
    else:
        (q_ref, k_ref, v_ref, a_ref, wa_ref, ba_ref, o_ref, st_ref) = refs
    C = GLA_CHUNK
    H = GLA_HEADS
    dk = q_ref.shape[1] // H
    dv = v_ref.shape[1] // H

    @pl.when(pl.program_id(1) == 0)
    def _():
        st_ref[...] = jnp.zeros_like(st_ref)

    ii = lax.broadcasted_iota(I32, (C, C), 0)
    jj = lax.broadcasted_iota(I32, (C, C), 1)
    causal = (ii <= jj) if reverse else (ii >= jj)
    tri = causal.astype(F32)
    a_lo = GLA_GATE_RANK if reverse else 0
    wa = wa_ref[...]
    ba = ba_ref[...]

    def body(ci, carry):
        c = (nchunks - 1 - ci) if reverse else ci
        sl = pl.ds(pl.multiple_of(c * C, C), C)
        a = a_ref[sl, :][:, a_lo:a_lo + GLA_GATE_RANK].astype(F32)
        logit = jnp.dot(a, wa, precision=lax.Precision.HIGHEST, preferred_element_type=F32) + ba
        log_sig = jnp.minimum(logit, 0.0) - jnp.log1p(jnp.exp(-jnp.abs(logit)))
        la = log_sig * (1.0 / GLA_TAU)
        bc_all = jnp.dot(tri, la, precision=lax.Precision.HIGHEST, preferred_element_type=F32)
        bl_all = jnp.sum(la, axis=0, keepdims=True)
        for h in range(H):
            kcol = slice(h * dk, (h + 1) * dk)
            vcol = slice(h * dv, (h + 1) * dv)
            bc = bc_all[:, kcol]
            bl = bl_all[:, kcol]
            q = q_ref[sl, kcol].astype(F32)
            k = k_ref[sl, kcol].astype(F32)
            v = v_ref[sl, vcol]
            q_in = (q * jnp.exp(bc)).astype(BF16)
            k_in = (k * jnp.exp(-bc)).astype(BF16)
            k_end = (k * jnp.exp(bl - bc)).astype(BF16)
            att = lax.dot_general(q_in, k_in, (((1,), (1,)), ((), ())), preferred_element_type=F32)
            att = jnp.where(causal, att, 0.0).astype(BF16)
            st = st_ref[h]
            o = jnp.dot(att, v, preferred_element_type=F32)
            o = o + lax.dot_general(q_in, st.astype(BF16), (((1,), (1,)), ((), ())),
                                    preferred_element_type=F32)
            st_ref[h] = st * jnp.exp(bl) + lax.dot_general(
                v, k_end, (((0,), (0,)), ((), ())), preferred_element_type=F32)
            if finalize:
                o = o + of_ref[sl, vcol].astype(F32)
                o = o * lax.rsqrt(jnp.mean(o * o, axis=-1, keepdims=True) + NORM_EPS) * gn_ref[...]
                r = r_ref[sl, vcol].astype(F32)
                o = o * (r * _sigmoid(r))
            o_ref[sl, vcol] = o.astype(o_ref.dtype)
        return carry

    lax.fori_loop(0, nchunks, body, 0, unroll=2)


def _gla_pass(z, a, w_a2, b_a, B, S, D, *, reverse, o_fwd=None, g_norm=None, tb=512):
    T = z.shape[0]
    H = GLA_HEADS
    DK = D // 2
    tb = _tile(S, tb)
    nb = S // tb
    finalize = o_fwd is not None
    d = 1 if reverse else 0

    def rb(b, i):
        return b * nb + ((nb - 1 - i) if reverse else i)

    in_specs = [pl.BlockSpec((tb, DK), lambda b, i: (rb(b, i), 0)),
                pl.BlockSpec((tb, DK), lambda b, i: (rb(b, i), 1)),
                pl.BlockSpec((tb, D), lambda b, i: (rb(b, i), 1)),
                pl.BlockSpec((tb, 2 * GLA_GATE_RANK), lambda b, i: (rb(b, i), 0)),
                pl.BlockSpec((None, GLA_GATE_RANK, DK), lambda b, i: (d, 0, 0)),
                pl.BlockSpec((None, 1, DK), lambda b, i: (d, 0, 0))]
    args = [z, z, z, a, w_a2, b_a.reshape(2, 1, DK)]
    if finalize:
        in_specs += [pl.BlockSpec((tb, D), lambda b, i: (rb(b, i), 0)),
                     pl.BlockSpec((tb, D), lambda b, i: (rb(b, i), 2)),
                     pl.BlockSpec((1, D // H), lambda b, i: (0, 0))]
        args += [o_fwd, z, g_norm.reshape(1, D // H)]
    return pl.pallas_call(
        functools.partial(_gla_kernel, reverse=reverse, finalize=finalize, nchunks=tb // GLA_CHUNK),
        grid=(B, nb),
        in_specs=in_specs,
        out_specs=pl.BlockSpec((tb, D), lambda b, i: (rb(b, i), 0)),
        out_shape=jax.ShapeDtypeStruct((T, D), BF16),
        scratch_shapes=[pltpu.VMEM((H, D // H, DK // H), F32)],
        compiler_params=_params(("parallel", "arbitrary")),
        name="gla_bwd_finalize" if finalize else "gla_fwd",
    )(*args)


def gla_mixer_core(z, a, w_a2, b_a, g_norm, B, S, D):
    o_f = _gla_pass(z, a, w_a2, b_a, B, S, D, reverse=False)
    return _gla_pass(z, a, w_a2, b_a, B, S, D, reverse=True, o_fwd=o_f, g_norm=g_norm)


def _pack_bf16_pairs(x):
    m = x.shape[1] // 2
    xb = x.astype(BF16).astype(F32)
    lo = pltpu.bitcast(xb[:, :m], U32) >> 16
    hi = pltpu.bitcast(xb[:, m:], U32)
    return hi | lo


def _unpack_bf16_pairs(p):
    lo = pltpu.bitcast(p << 16, F32)
    hi = pltpu.bitcast(p & jnp.uint32(0xFFFF0000), F32)
    return lo, hi


def _router_kernel(x_ref, g_ref, w_ref, b_ref, hp_ref, ri_ref, rw_ref, cnt_ref):
    h = _rms(x_ref[...], g_ref[...])
    hp_ref[...] = _pack_bf16_pairs(h)
    h_hi = h.astype(BF16)
    h_lo = (h - h_hi.astype(F32)).astype(BF16)
    parts = (jnp.dot(h_hi, w_ref[...], preferred_element_type=F32)
             + jnp.dot(h_lo, w_ref[...], preferred_element_type=F32))
    logits = parts[:, :LANES] + parts[:, LANES:] + b_ref[...]
    lane = lax.broadcasted_iota(I32, logits.shape, 1)
    lane_f = lane.astype(F32)
    neg = -jnp.inf

    def first_lane(hit):
        return jnp.min(jnp.where(hit, lane_f, float(LANES)), axis=-1, keepdims=True).astype(I32)

    gl = jnp.where(lane < MOE_GROUPS, logits, neg)
    gm = jnp.max(gl, axis=-1, keepdims=True)
    g_top = 1.0 / jnp.sum(jnp.exp(gl - gm), axis=-1, keepdims=True)
    g_idx = first_lane(gl == gm)
    e_lane = lane - MOE_GROUPS
    in_grp = (e_lane >= 0) & (e_lane < MOE_EXPERTS) & ((e_lane >> 3) == g_idx)
    el = jnp.where(in_grp, logits, neg)
    m1 = jnp.max(el, axis=-1, keepdims=True)
    es = jnp.sum(jnp.exp(el - m1), axis=-1, keepdims=True)
    i1 = first_lane(el == m1)
    el2 = jnp.where(lane == i1, neg, el)
    m2 = jnp.max(el2, axis=-1, keepdims=True)
    i2 = first_lane(el2 == m2)
    p1 = 1.0 / es
    p2 = jnp.exp(m2 - m1) / es
    w1 = g_top * p1 / (p1 + p2)
    w2 = g_top * p2 / (p1 + p2)
    e1 = i1 - MOE_GROUPS
    e2 = i2 - MOE_GROUPS
    tm = logits.shape[0]
    onehot = jnp.where((lane == e1) | (lane == e2), 1.0, 0.0)
    before = (lax.broadcasted_iota(I32, (tm, tm), 0) > lax.broadcasted_iota(I32, (tm, tm), 1))
    rank = jnp.dot(before.astype(BF16), onehot.astype(BF16), preferred_element_type=F32)
    r1 = jnp.sum(jnp.where(lane == e1, rank, 0.0), axis=-1, keepdims=True).astype(I32)
    r2 = jnp.sum(jnp.where(lane == e2, rank, 0.0), axis=-1, keepdims=True).astype(I32)
    ri_ref[...] = jnp.where(lane == 0, e1, jnp.where(lane == 1, e2,
                            jnp.where(lane == 2, r1, jnp.where(lane == 3, r2, 0))))
    rw_ref[...] = jnp.where(lane == 0, w1, jnp.where(lane == 1, w2, 0.0))
    cnt = jnp.sum(onehot, axis=0, keepdims=True).astype(I32)
    cnt_ref[...] = jnp.broadcast_to(cnt, cnt_ref.shape)


def moe_router(x, gain, w_router, b_router, tm=ROUTER_TILE):
    T, D = x.shape
    tm = _tile(T, tm)
    return pl.pallas_call(
        _router_kernel,
        grid=(T // tm,),
        in_specs=[pl.BlockSpec((tm, D), lambda i: (i, 0)),
                  pl.BlockSpec((1, D), lambda i: (0, 0)),
                  pl.BlockSpec((D, 2 * LANES), lambda i: (0, 0)),
                  pl.BlockSpec((1, LANES), lambda i: (0, 0))],
        out_specs=[pl.BlockSpec((tm, D // 2), lambda i: (i, 0)),
                   pl.BlockSpec((tm, LANES), lambda i: (i, 0)),
                   pl.BlockSpec((tm, LANES), lambda i: (i, 0)),
                   pl.BlockSpec((None, 8, LANES), lambda i: (i, 0, 0))],
        out_shape=[jax.ShapeDtypeStruct((T, D // 2), U32),
                   jax.ShapeDtypeStruct((T, LANES), I32),
                   jax.ShapeDtypeStruct((T, LANES), F32),
                   jax.ShapeDtypeStruct((T // tm, 8, LANES), I32)],
        compiler_params=_params(("parallel",)),
        name="moe_router",
    )(x, gain.reshape(1, D), w_router, b_router)


def _expert_kernel(be_ref, bv_ref, x_ref, wg_ref, wu_ref, wd_ref, o_ref):
    i = pl.program_id(0)
    nvalid = bv_ref[i]

    @pl.when(nvalid > 0)
    def _():
        row = lax.broadcasted_iota(I32, x_ref.shape, 0)
        xp = jnp.where(row < nvalid, x_ref[...], jnp.uint32(0))
        lo, hi = _unpack_bf16_pairs(xp)
        x = jnp.concatenate([lo.astype(BF16), hi.astype(BF16)], axis=1)
        a = jnp.dot(x, wg_ref[...], preferred_element_type=F32)
        u = jnp.dot(x, wu_ref[...], preferred_element_type=F32)
        hmid = (a * _sigmoid(a) * u).astype(BF16)
        y = jnp.dot(hmid, wd_ref[...], preferred_element_type=F32)
        o_ref[...] = _pack_bf16_pairs(y)

    @pl.when(nvalid == 0)
    def _():
        o_ref[...] = jnp.zeros_like(o_ref)


def moe_experts(xs, blk_e, blk_valid, w_gate, w_up, w_down, layer):
    n_rows, dh = xs.shape
    _, E, D, FF = w_gate.shape
    bm = MOE_ROW_BLOCK
    nblk = n_rows // bm
    grid_spec = pltpu.PrefetchScalarGridSpec(
        num_scalar_prefetch=2,
        grid=(nblk,),
        in_specs=[pl.BlockSpec((bm, dh), lambda i, be, bv: (i, 0)),
                  pl.BlockSpec((None, None, D, FF), lambda i, be, bv: (layer, be[i], 0, 0)),
                  pl.BlockSpec((None, None, D, FF), lambda i, be, bv: (layer, be[i], 0, 0)),
                  pl.BlockSpec((None, None, FF, D), lambda i, be, bv: (layer, be[i], 0, 0))],
        out_specs=pl.BlockSpec((bm, dh), lambda i, be, bv: (i, 0)),
    )
    return pl.pallas_call(
        _expert_kernel,
        grid_spec=grid_spec,
        out_shape=jax.ShapeDtypeStruct((n_rows, dh), U32),
        compiler_params=_params(("arbitrary",)),
        name="moe_experts",
    )(blk_e, blk_valid, xs, w_gate, w_up, w_down)


def _sc_mesh():
    return plsc.VectorSubcoreMesh(core_axis_name="c", subcore_axis_name="s")


def _sc_worker_id():
    return lax.axis_index("s") * SC_CORES + lax.axis_index("c")


def sc_scatter_rows(src, pos, n_rows):
    T, W = src.shape
    per_w = T // SC_WORKERS
    nch = per_w // SC_CHUNK
    assert per_w * SC_WORKERS == T and nch * SC_CHUNK == per_w
    idx = pos.reshape(SC_WORKERS, nch, SC_CHUNK, 2).transpose(0, 1, 3, 2).reshape(SC_WORKERS, 2 * nch, SC_CHUNK)

    @functools.partial(
        pl.kernel, mesh=_sc_mesh(),
        out_type=jax.ShapeDtypeStruct((n_rows, W), src.dtype),
        scratch_types=[pltpu.VMEM((2 * nch, SC_CHUNK), I32),
                       pltpu.VMEM((SC_CHUNK, W), src.dtype),
                       pltpu.SemaphoreType.DMA],
    )
    def k(src_hbm, idx_hbm, out_hbm, idx_v, rows_v, sem):
        wid = _sc_worker_id()
        base = wid * per_w
        pltpu.sync_copy(idx_hbm.at[wid], idx_v)

        @pl.loop(0, nch)
        def _(c):
            pltpu.sync_copy(src_hbm.at[pl.ds(base + c * SC_CHUNK, SC_CHUNK)], rows_v)
            pltpu.async_copy(rows_v, out_hbm.at[idx_v.at[2 * c]], sem).wait()
            pltpu.async_copy(rows_v, out_hbm.at[idx_v.at[2 * c + 1]], sem).wait()

    return k(src, idx)


def sc_gather_rows(table, idx_flat):
    N = idx_flat.shape[0]
    W = table.shape[1]
    per_w = N // SC_WORKERS
    nch = per_w // SC_CHUNK
    assert per_w * SC_WORKERS == N and nch * SC_CHUNK == per_w
    idx = idx_flat.reshape(SC_WORKERS, nch, SC_CHUNK)

    @functools.partial(
        pl.kernel, mesh=_sc_mesh(),
        out_type=jax.ShapeDtypeStruct((N, W), table.dtype),
        scratch_types=[pltpu.VMEM((nch, SC_CHUNK), I32),
                       pltpu.VMEM((SC_CHUNK, W), table.dtype),
                       pltpu.SemaphoreType.DMA],
    )
    def k(table_hbm, idx_hbm, out_hbm, idx_v, rows_v, sem):
        wid = _sc_worker_id()
        base = wid * per_w
        pltpu.sync_copy(idx_hbm.at[wid], idx_v)

        @pl.loop(0, nch)
        def _(c):
            pltpu.async_copy(table_hbm.at[idx_v.at[c]], rows_v, sem).wait()
            pltpu.sync_copy(rows_v, out_hbm.at[pl.ds(base + c * SC_CHUNK, SC_CHUNK)])

    return k(table, idx)


def _dispatch_plan(ri, tile_counts, bm):
    T = ri.shape[0]
    E = MOE_EXPERTS
    ntiles = tile_counts.shape[0]
    n_rows = (T * 2 // bm + E) * bm
    nblk = n_rows // bm
    eidx = ri[:, 0:2]
    tile_end = jnp.cumsum(tile_counts, axis=0)
    counts = tile_end[-1]
    padded = (counts + bm - 1) // bm * bm
    pad_end = jnp.cumsum(padded)
    base = pad_end - padded
    start = base[None, :] + tile_end - tile_counts
    experts = jnp.arange(E, dtype=I32)
    sel = eidx.reshape(ntiles, T // ntiles, 2, 1) == experts
    pos = jnp.sum(jnp.where(sel, start[:, None, None, :], 0), axis=-1).reshape(T, 2) + ri[:, 2:4]
    blk_start = jnp.arange(nblk, dtype=I32) * bm
    blk_e = jnp.minimum(jnp.sum((pad_end[None, :] <= blk_start[:, None]).astype(I32), axis=1), E - 1)
    cnt_end = jnp.sum(jnp.where(blk_e[:, None] == experts, (base + counts)[None, :], 0), axis=1)
    blk_valid = jnp.clip(cnt_end - blk_start, 0, bm).astype(I32)
    return pos.astype(I32), blk_e.astype(I32), blk_valid, n_rows


def _ple_kernel(x_ref, y_ref, rw_ref, p_ref, g_ref, wg_ref, wp_ref, *rest, sub, tn):
    o_ref = rest[-1]
    tm, D = x_ref.shape
    g = g_ref[...]
    for s0 in range(0, tm, sub):
        rows = pl.ds(s0, sub)
        rw = rw_ref[rows, :]
        lo0, hi0 = _unpack_bf16_pairs(y_ref[0, rows, :])
        lo1, hi1 = _unpack_bf16_pairs(y_ref[1, rows, :])
        w0 = rw[:, 0:1]
        w1 = rw[:, 1:2]
        y = jnp.concatenate([lo0 * w0 + lo1 * w1, hi0 * w0 + hi1 * w1], axis=1)
        x2 = x_ref[rows, :] + y
        hp = _rms(x2, g).astype(BF16)
        pb = p_ref[rows, :].astype(BF16)
        for c0 in range(0, D, tn):
            gate = _sigmoid(jnp.dot(hp, wg_ref[:, c0:c0 + tn], preferred_element_type=F32))
            proj = jnp.dot(pb, wp_ref[:, c0:c0 + tn], preferred_element_type=F32)
            o_ref[rows, c0:c0 + tn] = x2[:, c0:c0 + tn] + gate * proj
        if len(rest) == 2:
            o_ref[rows, :] = _rms(o_ref[rows, :], rest[0][...])


def moe_combine_ple(x, ysel, rw, p, gain, w_gate, w_proj, layer, final_gain=None,
                    tm=256, sub=128, tn=512):
    T, D = x.shape
    P = p.shape[2]
    tm, tn = _tile(T, tm), _tile(D, tn)
    sub = _tile(tm, sub)
    in_specs = [pl.BlockSpec((tm, D), lambda i: (i, 0)),
                pl.BlockSpec((2, tm, D // 2), lambda i: (0, i, 0)),
                pl.BlockSpec((tm, LANES), lambda i: (i, 0)),
                pl.BlockSpec((None, tm, P), lambda i: (layer, i, 0)),
                pl.BlockSpec((1, D), lambda i: (0, 0)),
                pl.BlockSpec((None, D, D), lambda i: (layer, 0, 0)),
                pl.BlockSpec((None, P, D), lambda i: (layer, 0, 0))]
    args = [x, ysel, rw, p, gain.reshape(1, D), w_gate, w_proj]
    if final_gain is not None:
        in_specs.append(pl.BlockSpec((1, D), lambda i: (0, 0)))
        args.append(final_gain.reshape(1, D))
    return pl.pallas_call(
        functools.partial(_ple_kernel, sub=sub, tn=tn),
        grid=(T // tm,),
        in_specs=in_specs,
        out_specs=pl.BlockSpec((tm, D), lambda i: (i, 0)),
        out_shape=jax.ShapeDtypeStruct((T, D), F32),
        compiler_params=_params(("parallel",)),
        name="moe_combine_ple",
    )(*args)


def _moe_layer(x, p, w, shared, layer, final_gain=None):
    T, D = x.shape
    hp, ri, rw, cnt = moe_router(x, w["norm_ffn"], w["w_router"], w["b_router"])
    pos, blk_e, blk_valid, n_rows = _dispatch_plan(ri, cnt[:, 0, :MOE_EXPERTS], MOE_ROW_BLOCK)
    xs = sc_scatter_rows(hp, pos, n_rows)
    ys = moe_experts(xs, blk_e, blk_valid, shared["moe_w_gate"], shared["moe_w_up"],
                     shared["moe_w_down"], layer)
    ysel = sc_gather_rows(ys, pos.T.reshape(-1)).reshape(2, T, D // 2)
    return moe_combine_ple(x, ysel, rw, p, w["norm_ple"], shared["ple_w_gate"],
                           shared["ple_w_proj"], layer, final_gain)


def _trunk(x, p, layers, shared, norm_final):
    B, S, D = x.shape
    x = x.reshape(B * S, D)
    p = p.reshape(p.shape[0], B * S, p.shape[-1])
    for i, w in enumerate(layers):
        kind = w["kind"]
        if kind == 0:
            qkv = norm_matmul(x, w["norm_mix"], w["w_qkv"], w["qkv_colscale"])
            o = na_attention(qkv, w["na_bias"], B, S)
            x = matmul_residual(o, w["w_o"], x)
        elif kind == 1:
            x = pool_mixer(x, w["norm_mix"], w["pool_w"], w["pool_scale"], B, S)
        else:
            z = norm_matmul(x, w["norm_mix"], w["w_in"], w["in_colscale"], tn=w["in_tn"])
            a = z[:, 3 * D:3 * D + 2 * GLA_GATE_RANK]
            o = gla_mixer_core(z, a, w["w_a2"], w["b_a"], w["g_norm"], B, S, D)
            x = matmul_residual(o, w["w_o"], x)
        last = i == len(layers) - 1
        x = _moe_layer(x, p, w, shared, i, norm_final if last else None)
    return x.reshape(B, S, D)


def _prepare_layers(norm_mix, norm_ffn, norm_ple, na_w_qkv, na_rpb, na_w_o, pool_w, pool_scale,
                    gla_w_in, gla_w_a2, gla_b_a, gla_norm, gla_w_o, moe_w_rg, moe_b_rg, moe_w_re,
                    moe_b_re, moe_w_gate, moe_w_up, moe_w_down, ple_w_proj, ple_w_gate):
    depth, D = norm_mix.shape
    layers = []
    for i in range(depth):
        j, kind = i // 3, i % 3
        w = {"kind": kind, "norm_mix": norm_mix[i], "norm_ffn": norm_ffn[i], "norm_ple": norm_ple[i]}
        if kind == 0:
            w["w_qkv"] = na_w_qkv[j].astype(BF16)
            w["qkv_colscale"] = jnp.concatenate(
                [jnp.full((D,), NA_HEAD_DIM ** -0.5 * LOG2E, F32), jnp.ones((2 * D,), F32)])
            w["na_bias"] = _na_bias_table(na_rpb[j])
            w["w_o"] = na_w_o[j].astype(BF16)
        elif kind == 1:
            w["pool_w"] = pool_w[j].astype(BF16)
            w["pool_scale"] = pool_scale[j]
        else:
            n_in = gla_w_in.shape[2]
            n_pad = -(-n_in // (7 * LANES)) * (7 * LANES)
            w["w_in"] = jnp.pad(gla_w_in[j], ((0, 0), (0, n_pad - n_in))).astype(BF16)
            dkh = D // 2 // GLA_HEADS
            w["in_colscale"] = jnp.concatenate(
                [jnp.full((D // 2,), dkh ** -0.5, F32), jnp.ones((n_pad - D // 2,), F32)])
            w["in_tn"] = n_pad // 7
            w["w_a2"] = gla_w_a2[j]
            w["b_a"] = gla_b_a[j]
            w["g_norm"] = gla_norm[j]
            w["w_o"] = gla_w_o[j].astype(BF16)
        wr = jnp.concatenate([moe_w_rg[i], moe_w_re[i]], axis=1)
        br = jnp.concatenate([moe_b_rg[i], moe_b_re[i]])
        wr = jnp.pad(wr, ((0, 0), (0, LANES - wr.shape[1])))
        wr_hi = wr.astype(BF16)
        wr_lo = (wr - wr_hi.astype(F32)).astype(BF16)
        w["w_router"] = jnp.concatenate([wr_hi, wr_lo], axis=1)
        w["b_router"] = jnp.pad(br, (0, LANES - br.shape[0])).reshape(1, LANES)
        layers.append(w)
    shared = {"moe_w_gate": moe_w_gate.astype(BF16), "moe_w_up": moe_w_up.astype(BF16),
              "moe_w_down": moe_w_down.astype(BF16), "ple_w_gate": ple_w_gate.astype(BF16),
              "ple_w_proj": ple_w_proj.astype(BF16)}
    return layers, shared


def kernel(x_prompt, x_sample, p_prompt, p_sample, norm_mix, norm_ffn, norm_ple, norm_final, na_w_qkv, na_rpb, na_w_o, pool_w, pool_scale, gla_w_in, gla_w_a2, gla_b_a, gla_norm, gla_w_o, moe_w_rg, moe_b_rg, moe_w_re, moe_b_re, moe_w_gate, moe_w_up, moe_w_down, ple_w_proj, ple_w_gate):
    layers, shared = _prepare_layers(
        norm_mix, norm_ffn, norm_ple, na_w_qkv, na_rpb, na_w_o, pool_w, pool_scale, gla_w_in, gla_w_a2,
        gla_b_a, gla_norm, gla_w_o, moe_w_rg, moe_b_rg, moe_w_re, moe_b_re, moe_w_gate, moe_w_up,
        moe_w_down, ple_w_proj, ple_w_gate)
    y_prompt = _trunk(x_prompt, p_prompt, layers, shared, norm_final)
    y_sample = _trunk(x_sample, p_sample, layers, shared, norm_final)
    return (y_prompt, y_sample)
```

```python
import functools

import numpy as np
import jax
import jax.numpy as jnp
from jax import lax
from jax.experimental import pallas as pl
from jax.experimental.pallas import tpu as pltpu
from jax.experimental.pallas import tpu_sc as plsc

F32 = jnp.float32
BF16 = jnp.bfloat16
I32 = jnp.int32
U32 = jnp.uint32

NORM_EPS = 1e-6
LOG2E = 1.4426950408889634
GRID_W = 64
NA_HEAD_DIM = 32
NA_WIN_H = 8
NA_WIN_W = 16
NA_HEADS_PER_STEP = 4
NA_ROW_UNROLL = 16
POOL_WINDOWS = (2, 4, 8, 16)
POOL_HALO = 16
GLA_HEADS = 4
GLA_GATE_RANK = 16
GLA_TAU = 16.0
GLA_CHUNK = 64
MOE_GROUPS = 4
MOE_PER_GROUP = 8
MOE_EXPERTS = MOE_GROUPS * MOE_PER_GROUP
MOE_ROW_BLOCK = 512
ROUTER_TILE = 512
LANES = 128
SC_CORES = 2
SC_WORKERS = SC_CORES * 16
SC_CHUNK = 32
VMEM_LIMIT = 56 * 1024 * 1024


def _tile(n, pref):
    t = min(n, pref)
    assert n % t == 0, (n, pref)
    return t


def _params(sem, vmem=VMEM_LIMIT):
    return pltpu.CompilerParams(dimension_semantics=sem, vmem_limit_bytes=vmem)


def _rms(x, g):
    ms = jnp.mean(x * x, axis=-1, keepdims=True)
    return x * lax.rsqrt(ms + NORM_EPS) * g


def _sigmoid(x):
    return 0.5 * jnp.tanh(0.5 * x) + 0.5


def _norm_mm_kernel(x_ref, g_ref, w_ref, cs_ref, o_ref, h_ref):
    @pl.when(pl.program_id(1) == 0)
    def _():
        h_ref[...] = _rms(x_ref[...], g_ref[...]).astype(BF16)

    acc = jnp.dot(h_ref[...], w_ref[...], preferred_element_type=F32)
    o_ref[...] = (acc * cs_ref[...]).astype(o_ref.dtype)


def norm_matmul(x, gain, w, colscale, out_dtype=BF16, tm=1024, tn=1024):
    T, D = x.shape
    N = w.shape[1]
    tm, tn = _tile(T, tm), _tile(N, tn)
    return pl.pallas_call(
        _norm_mm_kernel,
        grid=(T // tm, N // tn),
        in_specs=[pl.BlockSpec((tm, D), lambda i, j: (i, 0)),
                  pl.BlockSpec((1, D), lambda i, j: (0, 0)),
                  pl.BlockSpec((D, tn), lambda i, j: (0, j)),
                  pl.BlockSpec((1, tn), lambda i, j: (0, j))],
        out_specs=pl.BlockSpec((tm, tn), lambda i, j: (i, j)),
        out_shape=jax.ShapeDtypeStruct((T, N), out_dtype),
        scratch_shapes=[pltpu.VMEM((tm, D), BF16)],
        compiler_params=_params(("parallel", "arbitrary")),
        name="norm_matmul",
    )(x, gain.reshape(1, D), w, colscale.reshape(1, N))


def _mm_res_kernel(a_ref, w_ref, r_ref, o_ref, *, tn):
    a = a_ref[...]
    for c0 in range(0, o_ref.shape[1], tn):
        o_ref[:, c0:c0 + tn] = r_ref[:, c0:c0 + tn] + jnp.dot(
            a, w_ref[:, c0:c0 + tn], preferred_element_type=F32)


def matmul_residual(a, w, res, tm=512, tn=512):
    T, K = a.shape
    N = w.shape[1]
    tm, tn = _tile(T, tm), _tile(N, tn)
    return pl.pallas_call(
        functools.partial(_mm_res_kernel, tn=tn),
        grid=(T // tm,),
        in_specs=[pl.BlockSpec((tm, K), lambda i: (i, 0)),
                  pl.BlockSpec((K, N), lambda i: (0, 0)),
                  pl.BlockSpec((tm, N), lambda i: (i, 0))],
        out_specs=pl.BlockSpec((tm, N), lambda i: (i, 0)),
        out_shape=jax.ShapeDtypeStruct((T, N), F32),
        compiler_params=_params(("parallel",)),
        name="matmul_residual",
    )(a, w, res)


def _na_bias_table(rpb):
    H, nrow, ncol = rpb.shape
    G = H // NA_HEADS_PER_STEP
    assert ncol == 2 * NA_WIN_W - 1 and nrow == 2 * NA_WIN_H - 1 and 2 * GRID_W == LANES
    lead = GRID_W - NA_WIN_W
    rows = jnp.pad(rpb.astype(F32) * LOG2E, ((0, 0), (0, 0), (lead, LANES - lead - ncol)))
    q = np.arange(GRID_W)[:, None]
    c = np.arange(GRID_W)[None, :]
    wstart = np.clip(q - NA_WIN_W // 2, 0, GRID_W - NA_WIN_W)
    ok = (c >= wstart) & (c < wstart + NA_WIN_W)
    mask = np.where(np.concatenate([ok, ok], axis=1), 0.0, -np.inf).astype(np.float32)
    nq = NA_HEADS_PER_STEP * GRID_W
    nkeys = NA_WIN_H * GRID_W
    return pl.pallas_call(
        _na_bias_kernel,
        grid=(G,),
        in_specs=[pl.BlockSpec((NA_HEADS_PER_STEP, nrow, LANES), lambda g: (g, 0, 0)),
                  pl.BlockSpec((GRID_W, LANES), lambda g: (0, 0))],
        out_specs=pl.BlockSpec((None, NA_WIN_H, nq, nkeys), lambda g: (g, 0, 0, 0)),
        out_shape=jax.ShapeDtypeStruct((G, NA_WIN_H, nq, nkeys), F32),
        compiler_params=_params(("parallel",)),
        name="na_bias_table",
    )(rows, jnp.asarray(mask))


def _na_bias_kernel(r_ref, m_ref, o_ref):
    lane = lax.broadcasted_iota(I32, (GRID_W, LANES), 1)
    low = lane < GRID_W
    mask = m_ref[...]
    for dl in range(NA_WIN_H):
        for h in range(NA_HEADS_PER_STEP):
            for kp in range(NA_WIN_H // 2):
                drow = 2 * kp - dl + NA_WIN_H - 1
                r_even = jnp.broadcast_to(r_ref[h, pl.ds(drow, 1), :], (GRID_W, LANES))
                r_odd = jnp.broadcast_to(r_ref[h, pl.ds(drow + 1, 1), :], (GRID_W, LANES))
                t_even = pltpu.roll(r_even, GRID_W + 1, 1, stride=1, stride_axis=0)
                t_odd = pltpu.roll(r_odd, 1, 1, stride=1, stride_axis=0)
                o_ref[dl, pl.ds(h * GRID_W, GRID_W), pl.ds(kp * LANES, LANES)] = (
                    jnp.where(low, t_even, t_odd) + mask)


def _na_kernel(q_ref, k_ref, v_ref, b_ref, o_ref, *, rows):
    nkeys = NA_WIN_H * GRID_W
    lane_head = lax.broadcasted_iota(I32, (GRID_W, LANES), 1) // NA_HEAD_DIM
    masks = [lane_head == h for h in range(NA_HEADS_PER_STEP)]

    def body(r, carry):
        rs = jnp.clip(r - NA_WIN_H // 2, 0, rows - NA_WIN_H)
        q = q_ref[pl.ds(pl.multiple_of(r * GRID_W, GRID_W), GRID_W), :]
        kk = k_ref[pl.ds(pl.multiple_of(rs * GRID_W, GRID_W), nkeys), :]
        vv = v_ref[pl.ds(pl.multiple_of(rs * GRID_W, GRID_W), nkeys), :]
        zero = jnp.zeros_like(q)
        qm = jnp.concatenate([jnp.where(m, q, zero) for m in masks], axis=0)
        s = lax.dot_general(qm, kk, (((1,), (1,)), ((), ())), preferred_element_type=F32)
        s = s + b_ref[r - rs]
        m = jnp.max(s, axis=-1, keepdims=True)
        p = jnp.exp2((s - m).astype(BF16))
        v_ext = jnp.concatenate([vv, jnp.ones_like(vv)], axis=1)
        pv = jnp.dot(p, v_ext, preferred_element_type=F32)
        pv = pv[:, :LANES] / pv[:, LANES:]
        o = jnp.zeros((GRID_W, LANES), F32)
        for h in range(NA_HEADS_PER_STEP):
            o = jnp.where(masks[h], pv[h * GRID_W:(h + 1) * GRID_W], o)
        o_ref[pl.ds(pl.multiple_of(r * GRID_W, GRID_W), GRID_W), :] = o.astype(o_ref.dtype)
        return carry

    lax.fori_loop(0, rows, body, 0, unroll=NA_ROW_UNROLL)


def na_attention(qkv, bias, B, S):
    T, D3 = qkv.shape
    D = D3 // 3
    G = D // LANES
    rows = S // GRID_W
    assert rows >= NA_WIN_H and S % GRID_W == 0
    nq = NA_HEADS_PER_STEP * GRID_W
    nkeys = NA_WIN_H * GRID_W
    return pl.pallas_call(
        functools.partial(_na_kernel, rows=rows),
        grid=(G, B),
        in_specs=[pl.BlockSpec((S, LANES), lambda g, b: (b, g)),
                  pl.BlockSpec((S, LANES), lambda g, b: (b, G + g)),
                  pl.BlockSpec((S, LANES), lambda g, b: (b, 2 * G + g)),
                  pl.BlockSpec((None, NA_WIN_H, nq, nkeys), lambda g, b: (g, 0, 0, 0))],
        out_specs=pl.BlockSpec((S, LANES), lambda g, b: (b, g)),
        out_shape=jax.ShapeDtypeStruct((T, D), BF16),
        compiler_params=_params(("parallel", "parallel")),
        name="na_attention",
    )(qkv, qkv, qkv, bias)


def _pool_kernel(xp_ref, xc_ref, xn_ref, g_ref, w_ref, sc_ref, o_ref, h_ref, *, S, tm):
    i = pl.program_id(1)
    nt = pl.num_programs(1)
    g = g_ref[...]
    D = xc_ref.shape[1]
    dg = D // len(POOL_WINDOWS)
    xc = xc_ref[...]
    hc = _rms(xc, g)
    h_ref[pl.ds(0, POOL_HALO), :] = jnp.where(i > 0, _rms(xp_ref[...], g), 0.0)
    h_ref[pl.ds(POOL_HALO, tm), :] = hc
    h_ref[pl.ds(POOL_HALO + tm, POOL_HALO), :] = jnp.where(i < nt - 1, _rms(xn_ref[...], g), 0.0)
    t = i * tm + lax.broadcasted_iota(I32, (tm, 1), 0)
    for gi, win in enumerate(POOL_WINDOWS):
        c0 = gi * dg
        acc = h_ref[pl.ds(POOL_HALO - win // 2, tm), pl.ds(c0, dg)]
        for o in range(-win // 2 + 1, win // 2):
            acc = acc + h_ref[pl.ds(POOL_HALO + o, tm), pl.ds(c0, dg)]
        cnt = jnp.minimum(t + win // 2, S) - jnp.maximum(t - win // 2, 0)
        d = acc / cnt.astype(F32) - hc[:, c0:c0 + dg]
        y = jnp.dot(d.astype(BF16), w_ref[gi], preferred_element_type=F32)
        o_ref[:, pl.ds(c0, dg)] = xc[:, c0:c0 + dg] + y * sc_ref[:, pl.ds(c0, dg)]


def pool_mixer(x, gain, w_pool, scale, B, S, tm=256):
    T, D = x.shape
    tm = _tile(S, tm)
    nt = S // tm
    hb = tm // POOL_HALO
    nhb = S // POOL_HALO
    return pl.pallas_call(
        functools.partial(_pool_kernel, S=S, tm=tm),
        grid=(B, nt),
        in_specs=[pl.BlockSpec((POOL_HALO, D), lambda b, i: (b * nhb + jnp.maximum(i * hb - 1, 0), 0)),
                  pl.BlockSpec((tm, D), lambda b, i: (b * nt + i, 0)),
                  pl.BlockSpec((POOL_HALO, D),
                               lambda b, i: (b * nhb + jnp.minimum((i + 1) * hb, nhb - 1), 0)),
                  pl.BlockSpec((1, D), lambda b, i: (0, 0)),
                  pl.BlockSpec(w_pool.shape, lambda b, i: (0, 0, 0)),
                  pl.BlockSpec((1, D), lambda b, i: (0, 0))],
        out_specs=pl.BlockSpec((tm, D), lambda b, i: (b * nt + i, 0)),
        out_shape=jax.ShapeDtypeStruct((T, D), F32),
        scratch_shapes=[pltpu.VMEM((tm + 2 * POOL_HALO, D), F32)],
        compiler_params=_params(("parallel", "arbitrary")),
        name="pool_mixer",
    )(x, x, x, gain.reshape(1, D), w_pool, scale.reshape(1, D))


def _gla_kernel(*refs, reverse, finalize, nchunks):
    if finalize:
        (q_ref, k_ref, v_ref, a_ref, wa_ref, ba_ref, of_ref, r_ref, gn_ref, o_ref, st_ref) = refs
    else:
        (q_ref, k_ref, v_ref, a_ref, wa_ref, ba_ref, o_ref, st_ref) = refs
    C = GLA_CHUNK
    H = GLA_HEADS
    dk = q_ref.shape[1] // H
    dv = v_ref.shape[1] // H

    @pl.when(pl.program_id(1) == 0)
    def _():
        st_ref[...] = jnp.zeros_like(st_ref)

    ii = lax.broadcasted_iota(I32, (C, C), 0)
    jj = lax.broadcasted_iota(I32, (C, C), 1)
    causal = (ii <= jj) if reverse else (ii >= jj)
    tri = causal.astype(F32)
    a_lo = GLA_GATE_RANK if reverse else 0
    wa = wa_ref[...]
    ba = ba_ref[...]

    def body(ci, carry):
        c = (nchunks - 1 - ci) if reverse else ci
        sl = pl.ds(pl.multiple_of(c * C, C), C)
        a = a_ref[sl, :][:, a_lo:a_lo + GLA_GATE_RANK].astype(F32)
        logit = jnp.dot(a, wa, precision=lax.Precision.HIGHEST, preferred_element_type=F32) + ba
        log_sig = jnp.minimum(logit, 0.0) - jnp.log1p(jnp.exp(-jnp.abs(logit)))
        la = log_sig * (1.0 / GLA_TAU)
        bc_all = jnp.dot(tri, la, precision=lax.Precision.HIGHEST, preferred_element_type=F32)
        bl_all = jnp.sum(la, axis=0, keepdims=True)
        for h in range(H):
            kcol = slice(h * dk, (h + 1) * dk)
            vcol = slice(h * dv, (h + 1) * dv)
            bc = bc_all[:, kcol]
            bl = bl_all[:, kcol]
            q = q_ref[sl, kcol].astype(F32)
            k = k_ref[sl, kcol].astype(F32)
            v = v_ref[sl, vcol]
            q_in = (q * jnp.exp(bc)).astype(BF16)
            k_in = (k * jnp.exp(-bc)).astype(BF16)
            k_end = (k * jnp.exp(bl - bc)).astype(BF16)
            att = lax.dot_general(q_in, k_in, (((1,), (1,)), ((), ())), preferred_element_type=F32)
            att = jnp.where(causal, att, 0.0).astype(BF16)
            st = st_ref[h]
            o = jnp.dot(att, v, preferred_element_type=F32)
            o = o + lax.dot_general(q_in, st.astype(BF16), (((1,), (1,)), ((), ())),
                                    preferred_element_type=F32)
            st_ref[h] = st * jnp.exp(bl) + lax.dot_general(
                v, k_end, (((0,), (0,)), ((), ())), preferred_element_type=F32)
            if finalize:
                o = o + of_ref[sl, vcol].astype(F32)
                o = o * lax.rsqrt(jnp.mean(o * o, axis=-1, keepdims=True) + NORM_EPS) * gn_ref[...]
                r = r_ref[sl, vcol].astype(F32)
                o = o * (r * _sigmoid(r))
            o_ref[sl, vcol] = o.astype(o_ref.dtype)
        return carry

    lax.fori_loop(0, nchunks, body, 0, unroll=2)


def _gla_pass(z, a, w_a2, b_a, B, S, D, *, reverse, o_fwd=None, g_norm=None, tb=512):
    T = z.shape[0]
    H = GLA_HEADS
    DK = D // 2
    tb = _tile(S, tb)
    nb = S // tb
    finalize = o_fwd is not None
    d = 1 if reverse else 0

    def rb(b, i):
        return b * nb + ((nb - 1 - i) if reverse else i)

    in_specs = [pl.BlockSpec((tb, DK), lambda b, i: (rb(b, i), 0)),
                pl.BlockSpec((tb, DK), lambda b, i: (rb(b, i), 1)),
                pl.BlockSpec((tb, D), lambda b, i: (rb(b, i), 1)),
                pl.BlockSpec((tb, 2 * GLA_GATE_RANK), lambda b, i: (rb(b, i), 0)),
                pl.BlockSpec((None, GLA_GATE_RANK, DK), lambda b, i: (d, 0, 0)),
                pl.BlockSpec((None, 1, DK), lambda b, i: (d, 0, 0))]
    args = [z, z, z, a, w_a2, b_a.reshape(2, 1, DK)]
    if finalize:
        in_specs += [pl.BlockSpec((tb, D), lambda b, i: (rb(b, i), 0)),
                     pl.BlockSpec((tb, D), lambda b, i: (rb(b, i), 2)),
                     pl.BlockSpec((1, D // H), lambda b, i: (0, 0))]
        args += [o_fwd, z, g_norm.reshape(1, D // H)]
    return pl.pallas_call(
        functools.partial(_gla_kernel, reverse=reverse, finalize=finalize, nchunks=tb // GLA_CHUNK),
        grid=(B, nb),
        in_specs=in_specs,
        out_specs=pl.BlockSpec((tb, D), lambda b, i: (rb(b, i), 0)),
        out_shape=jax.ShapeDtypeStruct((T, D), BF16),
        scratch_shapes=[pltpu.VMEM((H, D // H, DK // H), F32)],
        compiler_params=_params(("parallel", "arbitrary")),
        name="gla_bwd_finalize" if finalize else "gla_fwd",
    )(*args)


def gla_mixer_core(z, a, w_a2, b_a, g_norm, B, S, D):
    o_f = _gla_pass(z, a, w_a2, b_a, B, S, D, reverse=False)
    return _gla_pass(z, a, w_a2, b_a, B, S, D, reverse=True, o_fwd=o_f, g_norm=g_norm)


def _pack_bf16_pairs(x):
    m = x.shape[1] // 2
    xb = x.astype(BF16).astype(F32)
    lo = pltpu.bitcast(xb[:, :m], U32) >> 16
    hi = pltpu.bitcast(xb[:, m:], U32)
    return hi | lo


def _unpack_bf16_pairs(p):
    lo = pltpu.bitcast(p << 16, F32)
    hi = pltpu.bitcast(p & jnp.uint32(0xFFFF0000), F32)
    return lo, hi


def _router_kernel(x_ref, g_ref, w_ref, b_ref, hp_ref, ri_ref, rw_ref, cnt_ref):
    h = _rms(x_ref[...], g_ref[...])
    hp_ref[...] = _pack_bf16_pairs(h)
    h_hi = h.astype(BF16)
    h_lo = (h - h_hi.astype(F32)).astype(BF16)
    parts = (jnp.dot(h_hi, w_ref[...], preferred_element_type=F32)
             + jnp.dot(h_lo, w_ref[...], preferred_element_type=F32))
    logits = parts[:, :LANES] + parts[:, LANES:] + b_ref[...]
    lane = lax.broadcasted_iota(I32, logits.shape, 1)
    lane_f = lane.astype(F32)
    neg = -jnp.inf

    def first_lane(hit):
        return jnp.min(jnp.where(hit, lane_f, float(LANES)), axis=-1, keepdims=True).astype(I32)

    gl = jnp.where(lane < MOE_GROUPS, logits, neg)
    gm = jnp.max(gl, axis=-1, keepdims=True)
    g_top = 1.0 / jnp.sum(jnp.exp(gl - gm), axis=-1, keepdims=True)
    g_idx = first_lane(gl == gm)
    e_lane = lane - MOE_GROUPS
    in_grp = (e_lane >= 0) & (e_lane < MOE_EXPERTS) & ((e_lane >> 3) == g_idx)
    el = jnp.where(in_grp, logits, neg)
    m1 = jnp.max(el, axis=-1, keepdims=True)
    es = jnp.sum(jnp.exp(el - m1), axis=-1, keepdims=True)
    i1 = first_lane(el == m1)
    el2 = jnp.where(lane == i1, neg, el)
    m2 = jnp.max(el2, axis=-1, keepdims=True)
    i2 = first_lane(el2 == m2)
    p1 = 1.0 / es
    p2 = jnp.exp(m2 - m1) / es
    w1 = g_top * p1 / (p1 + p2)
    w2 = g_top * p2 / (p1 + p2)
    e1 = i1 - MOE_GROUPS
    e2 = i2 - MOE_GROUPS
    tm = logits.shape[0]
    onehot = jnp.where((lane == e1) | (lane == e2), 1.0, 0.0)
    before = (lax.broadcasted_iota(I32, (tm, tm), 0) > lax.broadcasted_iota(I32, (tm, tm), 1))
    rank = jnp.dot(before.astype(BF16), onehot.astype(BF16), preferred_element_type=F32)
    r1 = jnp.sum(jnp.where(lane == e1, rank, 0.0), axis=-1, keepdims=True).astype(I32)
    r2 = jnp.sum(jnp.where(lane == e2, rank, 0.0), axis=-1, keepdims=True).astype(I32)
    ri_ref[...] = jnp.where(lane == 0, e1, jnp.where(lane == 1, e2,
                            jnp.where(lane == 2, r1, jnp.where(lane == 3, r2, 0))))
    rw_ref[...] = jnp.where(lane == 0, w1, jnp.where(lane == 1, w2, 0.0))
    cnt = jnp.sum(onehot, axis=0, keepdims=True).astype(I32)
    cnt_ref[...] = jnp.broadcast_to(cnt, cnt_ref.shape)


def moe_router(x, gain, w_router, b_router, tm=ROUTER_TILE):
    T, D = x.shape
    tm = _tile(T, tm)
    return pl.pallas_call(
        _router_kernel,
        grid=(T // tm,),
        in_specs=[pl.BlockSpec((tm, D), lambda i: (i, 0)),
                  pl.BlockSpec((1, D), lambda i: (0, 0)),
                  pl.BlockSpec((D, 2 * LANES), lambda i: (0, 0)),
                  pl.BlockSpec((1, LANES), lambda i: (0, 0))],
        out_specs=[pl.BlockSpec((tm, D // 2), lambda i: (i, 0)),
                   pl.BlockSpec((tm, LANES), lambda i: (i, 0)),
                   pl.BlockSpec((tm, LANES), lambda i: (i, 0)),
                   pl.BlockSpec((None, 8, LANES), lambda i: (i, 0, 0))],
        out_shape=[jax.ShapeDtypeStruct((T, D // 2), U32),
                   jax.ShapeDtypeStruct((T, LANES), I32),
                   jax.ShapeDtypeStruct((T, LANES), F32),
                   jax.ShapeDtypeStruct((T // tm, 8, LANES), I32)],
        compiler_params=_params(("parallel",)),
        name="moe_router",
    )(x, gain.reshape(1, D), w_router, b_router)


def _expert_kernel(be_ref, bv_ref, x_ref, wg_ref, wu_ref, wd_ref, o_ref):
    i = pl.program_id(0)
    nvalid = bv_ref[i]

    @pl.when(nvalid > 0)
    def _():
        row = lax.broadcasted_iota(I32, x_ref.shape, 0)
        xp = jnp.where(row < nvalid, x_ref[...], jnp.uint32(0))
        lo, hi = _unpack_bf16_pairs(xp)
        x = jnp.concatenate([lo.astype(BF16), hi.astype(BF16)], axis=1)
        a = jnp.dot(x, wg_ref[...], preferred_element_type=F32)
        u = jnp.dot(x, wu_ref[...], preferred_element_type=F32)
        hmid = (a * _sigmoid(a) * u).astype(BF16)
        y = jnp.dot(hmid, wd_ref[...], preferred_element_type=F32)
        o_ref[...] = _pack_bf16_pairs(y)

    @pl.when(nvalid == 0)
    def _():
        o_ref[...] = jnp.zeros_like(o_ref)


def moe_experts(xs, blk_e, blk_valid, w_gate, w_up, w_down, layer):
    n_rows, dh = xs.shape
    _, E, D, FF = w_gate.shape
    bm = MOE_ROW_BLOCK
    nblk = n_rows // bm
    grid_spec = pltpu.PrefetchScalarGridSpec(
        num_scalar_prefetch=2,
        grid=(nblk,),
        in_specs=[pl.BlockSpec((bm, dh), lambda i, be, bv: (i, 0)),
                  pl.BlockSpec((None, None, D, FF), lambda i, be, bv: (layer, be[i], 0, 0)),
                  pl.BlockSpec((None, None, D, FF), lambda i, be, bv: (layer, be[i], 0, 0)),
                  pl.BlockSpec((None, None, FF, D), lambda i, be, bv: (layer, be[i], 0, 0))],
        out_specs=pl.BlockSpec((bm, dh), lambda i, be, bv: (i, 0)),
    )
    return pl.pallas_call(
        _expert_kernel,
        grid_spec=grid_spec,
        out_shape=jax.ShapeDtypeStruct((n_rows, dh), U32),
        compiler_params=_params(("arbitrary",)),
        name="moe_experts",
    )(blk_e, blk_valid, xs, w_gate, w_up, w_down)


def _sc_mesh():
    return plsc.VectorSubcoreMesh(core_axis_name="c", subcore_axis_name="s")


def _sc_worker_id():
    return lax.axis_index("s") * SC_CORES + lax.axis_index("c")


def sc_scatter_rows(src, pos, n_rows):
    T, W = src.shape
    per_w = T // SC_WORKERS
    nch = per_w // SC_CHUNK
    assert per_w * SC_WORKERS == T and nch * SC_CHUNK == per_w
    idx = pos.reshape(SC_WORKERS, nch, SC_CHUNK, 2).transpose(0, 1, 3, 2).reshape(SC_WORKERS, 2 * nch, SC_CHUNK)

    @functools.partial(
        pl.kernel, mesh=_sc_mesh(),
        out_type=jax.ShapeDtypeStruct((n_rows, W), src.dtype),
        scratch_types=[pltpu.VMEM((2 * nch, SC_CHUNK), I32),
                       pltpu.VMEM((SC_CHUNK, W), src.dtype),
                       pltpu.SemaphoreType.DMA],
    )
    def k(src_hbm, idx_hbm, out_hbm, idx_v, rows_v, sem):
        wid = _sc_worker_id()
        base = wid * per_w
        pltpu.sync_copy(idx_hbm.at[wid], idx_v)

        @pl.loop(0, nch)
        def _(c):
            pltpu.sync_copy(src_hbm.at[pl.ds(base + c * SC_CHUNK, SC_CHUNK)], rows_v)
            pltpu.async_copy(rows_v, out_hbm.at[idx_v.at[2 * c]], sem).wait()
            pltpu.async_copy(rows_v, out_hbm.at[idx_v.at[2 * c + 1]], sem).wait()

    return k(src, idx)


def sc_gather_rows(table, idx_flat):
    N = idx_flat.shape[0]
    W = table.shape[1]
    per_w = N // SC_WORKERS
    nch = per_w // SC_CHUNK
    assert per_w * SC_WORKERS == N and nch * SC_CHUNK == per_w
    idx = idx_flat.reshape(SC_WORKERS, nch, SC_CHUNK)

    @functools.partial(
        pl.kernel, mesh=_sc_mesh(),
        out_type=jax.ShapeDtypeStruct((N, W), table.dtype),
        scratch_types=[pltpu.VMEM((nch, SC_CHUNK), I32),
                       pltpu.VMEM((SC_CHUNK, W), table.dtype),
                       pltpu.SemaphoreType.DMA],
    )
    def k(table_hbm, idx_hbm, out_hbm, idx_v, rows_v, sem):
        wid = _sc_worker_id()
        base = wid * per_w
        pltpu.sync_copy(idx_hbm.at[wid], idx_v)

        @pl.loop(0, nch)
        def _(c):
            pltpu.async_copy(table_hbm.at[idx_v.at[c]], rows_v, sem).wait()
            pltpu.sync_copy(rows_v, out_hbm.at[pl.ds(base + c * SC_CHUNK, SC_CHUNK)])

    return k(table, idx)


def _dispatch_plan(ri, tile_counts, bm):
    T = ri.shape[0]
    E = MOE_EXPERTS
    ntiles = tile_counts.shape[0]
    n_rows = (T * 2 // bm + E) * bm
    nblk = n_rows // bm
    eidx = ri[:, 0:2]
    tile_end = jnp.cumsum(tile_counts, axis=0)
    counts = tile_end[-1]
    padded = (counts + bm - 1) // bm * bm
    pad_end = jnp.cumsum(padded)
    base = pad_end - padded
    start = base[None, :] + tile_end - tile_counts
    experts = jnp.arange(E, dtype=I32)
    sel = eidx.reshape(ntiles, T // ntiles, 2, 1) == experts
    pos = jnp.sum(jnp.where(sel, start[:, None, None, :], 0), axis=-1).reshape(T, 2) + ri[:, 2:4]
    blk_start = jnp.arange(nblk, dtype=I32) * bm
    blk_e = jnp.minimum(jnp.sum((pad_end[None, :] <= blk_start[:, None]).astype(I32), axis=1), E - 1)
    cnt_end = jnp.sum(jnp.where(blk_e[:, None] == experts, (base + counts)[None, :], 0), axis=1)
    blk_valid = jnp.clip(cnt_end - blk_start, 0, bm).astype(I32)
    return pos.astype(I32), blk_e.astype(I32), blk_valid, n_rows


def _ple_kernel(x_ref, y_ref, rw_ref, p_ref, g_ref, wg_ref, wp_ref, *rest, sub, tn):
    o_ref = rest[-1]
    tm, D = x_ref.shape
    g = g_ref[...]
    for s0 in range(0, tm, sub):
        rows = pl.ds(s0, sub)
        rw = rw_ref[rows, :]
        lo0, hi0 = _unpack_bf16_pairs(y_ref[0, rows, :])
        lo1, hi1 = _unpack_bf16_pairs(y_ref[1, rows, :])
        w0 = rw[:, 0:1]
        w1 = rw[:, 1:2]
        y = jnp.concatenate([lo0 * w0 + lo1 * w1, hi0 * w0 + hi1 * w1], axis=1)
        x2 = x_ref[rows, :] + y
        hp = _rms(x2, g).astype(BF16)
        pb = p_ref[rows, :].astype(BF16)
        for c0 in range(0, D, tn):
            gate = _sigmoid(jnp.dot(hp, wg_ref[:, c0:c0 + tn], preferred_element_type=F32))
            proj = jnp.dot(pb, wp_ref[:, c0:c0 + tn], preferred_element_type=F32)
            o_ref[rows, c0:c0 + tn] = x2[:, c0:c0 + tn] + gate * proj
        if len(rest) == 2:
            o_ref[rows, :] = _rms(o_ref[rows, :], rest[0][...])


def moe_combine_ple(x, ysel, rw, p, gain, w_gate, w_proj, layer, final_gain=None,
                    tm=512, sub=256, tn=512):
    T, D = x.shape
    P = p.shape[2]
    tm, tn = _tile(T, tm), _tile(D, tn)
    sub = _tile(tm, sub)
    in_specs = [pl.BlockSpec((tm, D), lambda i: (i, 0)),
                pl.BlockSpec((2, tm, D // 2), lambda i: (0, i, 0)),
                pl.BlockSpec((tm, LANES), lambda i: (i, 0)),
                pl.BlockSpec((None, tm, P), lambda i: (layer, i, 0)),
                pl.BlockSpec((1, D), lambda i: (0, 0)),
                pl.BlockSpec((None, D, D), lambda i: (layer, 0, 0)),
                pl.BlockSpec((None, P, D), lambda i: (layer, 0, 0))]
    args = [x, ysel, rw, p, gain.reshape(1, D), w_gate, w_proj]
    if final_gain is not None:
        in_specs.append(pl.BlockSpec((1, D), lambda i: (0, 0)))
        args.append(final_gain.reshape(1, D))
    return pl.pallas_call(
        functools.partial(_ple_kernel, sub=sub, tn=tn),
        grid=(T // tm,),
        in_specs=in_specs,
        out_specs=pl.BlockSpec((tm, D), lambda i: (i, 0)),
        out_shape=jax.ShapeDtypeStruct((T, D), F32),
        compiler_params=_params(("parallel",)),
        name="moe_combine_ple",
    )(*args)


def _moe_layer(x, p, w, shared, layer, final_gain=None):
    T, D = x.shape
    hp, ri, rw, cnt = moe_router(x, w["norm_ffn"], w["w_router"], w["b_router"])
    pos, blk_e, blk_valid, n_rows = _dispatch_plan(ri, cnt[:, 0, :MOE_EXPERTS], MOE_ROW_BLOCK)
    xs = sc_scatter_rows(hp, pos, n_rows)
    ys = moe_experts(xs, blk_e, blk_valid, shared["moe_w_gate"], shared["moe_w_up"],
                     shared["moe_w_down"], layer)
    ysel = sc_gather_rows(ys, pos.T.reshape(-1)).reshape(2, T, D // 2)
    return moe_combine_ple(x, ysel, rw, p, w["norm_ple"], shared["ple_w_gate"],
                           shared["ple_w_proj"], layer, final_gain)


def _trunk(x, p, layers, shared, norm_final):
    B, S, D = x.shape
    x = x.reshape(B * S, D)
    p = p.reshape(p.shape[0], B * S, p.shape[-1])
    for i, w in enumerate(layers):
        kind = w["kind"]
        if kind == 0:
            qkv = norm_matmul(x, w["norm_mix"], w["w_qkv"], w["qkv_colscale"])
            o = na_attention(qkv, w["na_bias"], B, S)
            x = matmul_residual(o, w["w_o"], x)
        elif kind == 1:
            x = pool_mixer(x, w["norm_mix"], w["pool_w"], w["pool_scale"], B, S)
        else:
            z = norm_matmul(x, w["norm_mix"], w["w_in"], w["in_colscale"], tn=w["in_tn"])
            a = z[:, 3 * D:3 * D + 2 * GLA_GATE_RANK]
            o = gla_mixer_core(z, a, w["w_a2"], w["b_a"], w["g_norm"], B, S, D)
            x = matmul_residual(o, w["w_o"], x)
        last = i == len(layers) - 1
        x = _moe_layer(x, p, w, shared, i, norm_final if last else None)
    return x.reshape(B, S, D)


def _prepare_layers(norm_mix, norm_ffn, norm_ple, na_w_qkv, na_rpb, na_w_o, pool_w, pool_scale,
                    gla_w_in, gla_w_a2, gla_b_a, gla_norm, gla_w_o, moe_w_rg, moe_b_rg, moe_w_re,
                    moe_b_re, moe_w_gate, moe_w_up, moe_w_down, ple_w_proj, ple_w_gate):
    depth, D = norm_mix.shape
    layers = []
    for i in range(depth):
        j, kind = i // 3, i % 3
        w = {"kind": kind, "norm_mix": norm_mix[i], "norm_ffn": norm_ffn[i], "norm_ple": norm_ple[i]}
        if kind == 0:
            w["w_qkv"] = na_w_qkv[j].astype(BF16)
            w["qkv_colscale"] = jnp.concatenate(
                [jnp.full((D,), NA_HEAD_DIM ** -0.5 * LOG2E, F32), jnp.ones((2 * D,), F32)])
            w["na_bias"] = _na_bias_table(na_rpb[j])
            w["w_o"] = na_w_o[j].astype(BF16)
        elif kind == 1:
            w["pool_w"] = pool_w[j].astype(BF16)
            w["pool_scale"] = pool_scale[j]
        else:
            n_in = gla_w_in.shape[2]
            n_pad = -(-n_in // (7 * LANES)) * (7 * LANES)
            w["w_in"] = jnp.pad(gla_w_in[j], ((0, 0), (0, n_pad - n_in))).astype(BF16)
            dkh = D // 2 // GLA_HEADS
            w["in_colscale"] = jnp.concatenate(
                [jnp.full((D // 2,), dkh ** -0.5, F32), jnp.ones((n_pad - D // 2,), F32)])
            w["in_tn"] = n_pad // 7
            w["w_a2"] = gla_w_a2[j]
            w["b_a"] = gla_b_a[j]
            w["g_norm"] = gla_norm[j]
            w["w_o"] = gla_w_o[j].astype(BF16)
        wr = jnp.concatenate([moe_w_rg[i], moe_w_re[i]], axis=1)
        br = jnp.concatenate([moe_b_rg[i], moe_b_re[i]])
        wr = jnp.pad(wr, ((0, 0), (0, LANES - wr.shape[1])))
        wr_hi = wr.astype(BF16)
        wr_lo = (wr - wr_hi.astype(F32)).astype(BF16)
        w["w_router"] = jnp.concatenate([wr_hi, wr_lo], axis=1)
        w["b_router"] = jnp.pad(br, (0, LANES - br.shape[0])).reshape(1, LANES)
        layers.append(w)
    shared = {"moe_w_gate": moe_w_gate.astype(BF16), "moe_w_up": moe_w_up.astype(BF16),
              "moe_w_down": moe_w_down.astype(BF16), "ple_w_gate": ple_w_gate.astype(BF16),
              "ple_w_proj": ple_w_proj.astype(BF16)}
    return layers, shared


def kernel(x_prompt, x_sample, p_prompt, p_sample, norm_mix, norm_ffn, norm_ple, norm_final, na_w_qkv, na_rpb, na_w_o, pool_w, pool_scale, gla_w_in, gla_w_a2, gla_b_a, gla_norm, gla_w_o, moe_w_rg, moe_b_rg, moe_w_re, moe_b_re, moe_w_gate, moe_w_up, moe_w_down, ple_w_proj, ple_w_gate):
    layers, shared = _prepare_layers(
        norm_mix, norm_ffn, norm_ple, na_w_qkv, na_rpb, na_w_o, pool_w, pool_scale, gla_w_in, gla_w_a2,
        gla_b_a, gla_norm, gla_w_o, moe_w_rg, moe_b_rg, moe_w_re, moe_b_re, moe_w_gate, moe_w_up,
        moe_w_down, ple_w_proj, ple_w_gate)
    y_prompt = _trunk(x_prompt, p_prompt, layers, shared, norm_final)
    y_sample = _trunk(x_sample, p_sample, layers, shared, norm_final)
    return (y_prompt, y_sample)
```

```python
import functools

import numpy as np
import jax
import jax.numpy as jnp
from jax import lax
from jax.experimental import pallas as pl
from jax.experimental.pallas import tpu as pltpu
from jax.experimental.pallas import tpu_sc as plsc

F32 = jnp.float32
BF16 = jnp.bfloat16
I32 = jnp.int32
U32 = jnp.uint32

NORM_EPS = 1e-6
LOG2E = 1.4426950408889634
GRID_W = 64
NA_HEAD_DIM = 32
NA_WIN_H = 8
NA_WIN_W = 16
NA_HEADS_PER_STEP = 4
NA_ROW_UNROLL = 16
POOL_WINDOWS = (2, 4, 8, 16)
POOL_HALO = 16
GLA_HEADS = 4
GLA_GATE_RANK = 16
GLA_TAU = 16.0
GLA_CHUNK = 64
MOE_GROUPS = 4
MOE_PER_GROUP = 8
MOE_EXPERTS = MOE_GROUPS * MOE_PER_GROUP
MOE_ROW_BLOCK = 512
ROUTER_TILE = 512
LANES = 128
SC_CORES = 2
SC_WORKERS = SC_CORES * 16
SC_CHUNK = 32
VMEM_LIMIT = 56 * 1024 * 1024


def _tile(n, pref):
    t = min(n, pref)
    assert n % t == 0, (n, pref)
    return t


def _params(sem, vmem=VMEM_LIMIT):
    return pltpu.CompilerParams(dimension_semantics=sem, vmem_limit_bytes=vmem)


def _rms(x, g):
    ms = jnp.mean(x * x, axis=-1, keepdims=True)
    return x * lax.rsqrt(ms + NORM_EPS) * g


def _sigmoid(x):
    return 0.5 * jnp.tanh(0.5 * x) + 0.5


def _norm_mm_kernel(x_ref, g_ref, w_ref, cs_ref, o_ref, h_ref):
    @pl.when(pl.program_id(1) == 0)
    def _():
        h_ref[...] = _rms(x_ref[...], g_ref[...]).astype(BF16)

    acc = jnp.dot(h_ref[...], w_ref[...], preferred_element_type=F32)
    o_ref[...] = (acc * cs_ref[...]).astype(o_ref.dtype)


def norm_matmul(x, gain, w, colscale, out_dtype=BF16, tm=1024, tn=1024):
    T, D = x.shape
    N = w.shape[1]
    tm, tn = _tile(T, tm), _tile(N, tn)
    return pl.pallas_call(
        _norm_mm_kernel,
        grid=(T // tm, N // tn),
        in_specs=[pl.BlockSpec((tm, D), lambda i, j: (i, 0)),
                  pl.BlockSpec((1, D), lambda i, j: (0, 0)),
                  pl.BlockSpec((D, tn), lambda i, j: (0, j)),
                  pl.BlockSpec((1, tn), lambda i, j: (0, j))],
        out_specs=pl.BlockSpec((tm, tn), lambda i, j: (i, j)),
        out_shape=jax.ShapeDtypeStruct((T, N), out_dtype),
        scratch_shapes=[pltpu.VMEM((tm, D), BF16)],
        compiler_params=_params(("parallel", "arbitrary")),
        name="norm_matmul",
    )(x, gain.reshape(1, D), w, colscale.reshape(1, N))


def _mm_res_kernel(a_ref, w_ref, r_ref, o_ref, *, tn):
    a = a_ref[...]
    for c0 in range(0, o_ref.shape[1], tn):
        o_ref[:, c0:c0 + tn] = r_ref[:, c0:c0 + tn] + jnp.dot(
            a, w_ref[:, c0:c0 + tn], preferred_element_type=F32)


def matmul_residual(a, w, res, tm=512, tn=512):
    T, K = a.shape
    N = w.shape[1]
    tm, tn = _tile(T, tm), _tile(N, tn)
    return pl.pallas_call(
        functools.partial(_mm_res_kernel, tn=tn),
        grid=(T // tm,),
        in_specs=[pl.BlockSpec((tm, K), lambda i: (i, 0)),
                  pl.BlockSpec((K, N), lambda i: (0, 0)),
                  pl.BlockSpec((tm, N), lambda i: (i, 0))],
        out_specs=pl.BlockSpec((tm, N), lambda i: (i, 0)),
        out_shape=jax.ShapeDtypeStruct((T, N), F32),
        compiler_params=_params(("parallel",)),
        name="matmul_residual",
    )(a, w, res)


def _na_bias_table(rpb):
    H, nrow, ncol = rpb.shape
    G = H // NA_HEADS_PER_STEP
    assert ncol == 2 * NA_WIN_W - 1 and nrow == 2 * NA_WIN_H - 1 and 2 * GRID_W == LANES
    lead = GRID_W - NA_WIN_W
    rows = jnp.pad(rpb.astype(F32) * LOG2E, ((0, 0), (0, 0), (lead, LANES - lead - ncol)))
    q = np.arange(GRID_W)[:, None]
    c = np.arange(GRID_W)[None, :]
    wstart = np.clip(q - NA_WIN_W // 2, 0, GRID_W - NA_WIN_W)
    ok = (c >= wstart) & (c < wstart + NA_WIN_W)
    mask = np.where(np.concatenate([ok, ok], axis=1), 0.0, -np.inf).astype(np.float32)
    nq = NA_HEADS_PER_STEP * GRID_W
    nkeys = NA_WIN_H * GRID_W
    return pl.pallas_call(
        _na_bias_kernel,
        grid=(G,),
        in_specs=[pl.BlockSpec((NA_HEADS_PER_STEP, nrow, LANES), lambda g: (g, 0, 0)),
                  pl.BlockSpec((GRID_W, LANES), lambda g: (0, 0))],
        out_specs=pl.BlockSpec((None, NA_WIN_H, nq, nkeys), lambda g: (g, 0, 0, 0)),
        out_shape=jax.ShapeDtypeStruct((G, NA_WIN_H, nq, nkeys), F32),
        compiler_params=_params(("parallel",)),
        name="na_bias_table",
    )(rows, jnp.asarray(mask))


def _na_bias_kernel(r_ref, m_ref, o_ref):
    lane = lax.broadcasted_iota(I32, (GRID_W, LANES), 1)
    low = lane < GRID_W
    mask = m_ref[...]
    for dl in range(NA_WIN_H):
        for h in range(NA_HEADS_PER_STEP):
            for kp in range(NA_WIN_H // 2):
                drow = 2 * kp - dl + NA_WIN_H - 1
                r_even = jnp.broadcast_to(r_ref[h, pl.ds(drow, 1), :], (GRID_W, LANES))
                r_odd = jnp.broadcast_to(r_ref[h, pl.ds(drow + 1, 1), :], (GRID_W, LANES))
                t_even = pltpu.roll(r_even, GRID_W + 1, 1, stride=1, stride_axis=0)
                t_odd = pltpu.roll(r_odd, 1, 1, stride=1, stride_axis=0)
                o_ref[dl, pl.ds(h * GRID_W, GRID_W), pl.ds(kp * LANES, LANES)] = (
                    jnp.where(low, t_even, t_odd) + mask)


def _na_kernel(q_ref, k_ref, v_ref, b_ref, o_ref, *, rows):
    nkeys = NA_WIN_H * GRID_W
    lane_head = lax.broadcasted_iota(I32, (GRID_W, LANES), 1) // NA_HEAD_DIM
    masks = [lane_head == h for h in range(NA_HEADS_PER_STEP)]

    def body(r, carry):
        rs = jnp.clip(r - NA_WIN_H // 2, 0, rows - NA_WIN_H)
        q = q_ref[pl.ds(pl.multiple_of(r * GRID_W, GRID_W), GRID_W), :]
        kk = k_ref[pl.ds(pl.multiple_of(rs * GRID_W, GRID_W), nkeys), :]
        vv = v_ref[pl.ds(pl.multiple_of(rs * GRID_W, GRID_W), nkeys), :]
        zero = jnp.zeros_like(q)
        qm = jnp.concatenate([jnp.where(m, q, zero) for m in masks], axis=0)
        s = lax.dot_general(qm, kk, (((1,), (1,)), ((), ())), preferred_element_type=F32)
        s = s + b_ref[r - rs]
        m = jnp.max(s, axis=-1, keepdims=True)
        p = jnp.exp2((s - m).astype(BF16))
        v_ext = jnp.concatenate([vv, jnp.ones_like(vv)], axis=1)
        pv = jnp.dot(p, v_ext, preferred_element_type=F32)
        pv = pv[:, :LANES] / pv[:, LANES:]
        o = jnp.zeros((GRID_W, LANES), F32)
        for h in range(NA_HEADS_PER_STEP):
            o = jnp.where(masks[h], pv[h * GRID_W:(h + 1) * GRID_W], o)
        o_ref[pl.ds(pl.multiple_of(r * GRID_W, GRID_W), GRID_W), :] = o.astype(o_ref.dtype)
        return carry

    lax.fori_loop(0, rows, body, 0, unroll=NA_ROW_UNROLL)


def na_attention(qkv, bias, B, S):
    T, D3 = qkv.shape
    D = D3 // 3
    G = D // LANES
    rows = S // GRID_W
    assert rows >= NA_WIN_H and S % GRID_W == 0
    nq = NA_HEADS_PER_STEP * GRID_W
    nkeys = NA_WIN_H * GRID_W
    return pl.pallas_call(
        functools.partial(_na_kernel, rows=rows),
        grid=(G, B),
        in_specs=[pl.BlockSpec((S, LANES), lambda g, b: (b, g)),
                  pl.BlockSpec((S, LANES), lambda g, b: (b, G + g)),
                  pl.BlockSpec((S, LANES), lambda g, b: (b, 2 * G + g)),
                  pl.BlockSpec((None, NA_WIN_H, nq, nkeys), lambda g, b: (g, 0, 0, 0))],
        out_specs=pl.BlockSpec((S, LANES), lambda g, b: (b, g)),
        out_shape=jax.ShapeDtypeStruct((T, D), BF16),
        compiler_params=_params(("parallel", "parallel")),
        name="na_attention",
    )(qkv, qkv, qkv, bias)


def _pool_kernel(xp_ref, xc_ref, xn_ref, g_ref, w_ref, sc_ref, o_ref, h_ref, *, S, tm):
    i = pl.program_id(1)
    nt = pl.num_programs(1)
    g = g_ref[...]
    D = xc_ref.shape[1]
    dg = D // len(POOL_WINDOWS)
    xc = xc_ref[...]
    hc = _rms(xc, g)
    h_ref[pl.ds(0, POOL_HALO), :] = jnp.where(i > 0, _rms(xp_ref[...], g), 0.0)
    h_ref[pl.ds(POOL_HALO, tm), :] = hc
    h_ref[pl.ds(POOL_HALO + tm, POOL_HALO), :] = jnp.where(i < nt - 1, _rms(xn_ref[...], g), 0.0)
    t = i * tm + lax.broadcasted_iota(I32, (tm, 1), 0)
    for gi, win in enumerate(POOL_WINDOWS):
        c0 = gi * dg
        acc = h_ref[pl.ds(POOL_HALO - win // 2, tm), pl.ds(c0, dg)]
        for o in range(-win // 2 + 1, win // 2):
            acc = acc + h_ref[pl.ds(POOL_HALO + o, tm), pl.ds(c0, dg)]
        cnt = jnp.minimum(t + win // 2, S) - jnp.maximum(t - win // 2, 0)
        d = acc / cnt.astype(F32) - hc[:, c0:c0 + dg]
        y = jnp.dot(d.astype(BF16), w_ref[gi], preferred_element_type=F32)
        o_ref[:, pl.ds(c0, dg)] = xc[:, c0:c0 + dg] + y * sc_ref[:, pl.ds(c0, dg)]


def pool_mixer(x, gain, w_pool, scale, B, S, tm=256):
    T, D = x.shape
    tm = _tile(S, tm)
    nt = S // tm
    hb = tm // POOL_HALO
    nhb = S // POOL_HALO
    return pl.pallas_call(
        functools.partial(_pool_kernel, S=S, tm=tm),
        grid=(B, nt),
        in_specs=[pl.BlockSpec((POOL_HALO, D), lambda b, i: (b * nhb + jnp.maximum(i * hb - 1, 0), 0)),
                  pl.BlockSpec((tm, D), lambda b, i: (b * nt + i, 0)),
                  pl.BlockSpec((POOL_HALO, D),
                               lambda b, i: (b * nhb + jnp.minimum((i + 1) * hb, nhb - 1), 0)),
                  pl.BlockSpec((1, D), lambda b, i: (0, 0)),
                  pl.BlockSpec(w_pool.shape, lambda b, i: (0, 0, 0)),
                  pl.BlockSpec((1, D), lambda b, i: (0, 0))],
        out_specs=pl.BlockSpec((tm, D), lambda b, i: (b * nt + i, 0)),
        out_shape=jax.ShapeDtypeStruct((T, D), F32),
        scratch_shapes=[pltpu.VMEM((tm + 2 * POOL_HALO, D), F32)],
        compiler_params=_params(("parallel", "arbitrary")),
        name="pool_mixer",
    )(x, x, x, gain.reshape(1, D), w_pool, scale.reshape(1, D))


def _gla_kernel(*refs, reverse, finalize, nchunks):
    if finalize:
        (q_ref, k_ref, v_ref, a_ref, wa_ref, ba_ref, of_ref, r_ref, gn_ref, o_ref, st_ref) = refs
    else:
        (q_ref, k_ref, v_ref, a_ref, wa_ref, ba_ref, o_ref, st_ref) = refs
    C = GLA_CHUNK
    H = GLA_HEADS
    dk = q_ref.shape[1] // H
    dv = v_ref.shape[1] // H

    @pl.when(pl.program_id(1) == 0)
    def _():
        st_ref[...] = jnp.zeros_like(st_ref)

    ii = lax.broadcasted_iota(I32, (C, C), 0)
    jj = lax.broadcasted_iota(I32, (C, C), 1)
    causal = (ii <= jj) if reverse else (ii >= jj)
    tri = causal.astype(F32)
    a_lo = GLA_GATE_RANK if reverse else 0
    wa = wa_ref[...]
    ba = ba_ref[...]

    def body(ci, carry):
        c = (nchunks - 1 - ci) if reverse else ci
        sl = pl.ds(pl.multiple_of(c * C, C), C)
        a = a_ref[sl, :][:, a_lo:a_lo + GLA_GATE_RANK].astype(F32)
        logit = jnp.dot(a, wa, precision=lax.Precision.HIGHEST, preferred_element_type=F32) + ba
        log_sig = jnp.minimum(logit, 0.0) - jnp.log1p(jnp.exp(-jnp.abs(logit)))
        la = log_sig * (1.0 / GLA_TAU)
        bc_all = jnp.dot(tri, la, precision=lax.Precision.HIGHEST, preferred_element_type=F32)
        bl_all = jnp.sum(la, axis=0, keepdims=True)
        for h in range(H):
            kcol = slice(h * dk, (h + 1) * dk)
            vcol = slice(h * dv, (h + 1) * dv)
            bc = bc_all[:, kcol]
            bl = bl_all[:, kcol]
            q = q_ref[sl, kcol].astype(F32)
            k = k_ref[sl, kcol].astype(F32)
            v = v_ref[sl, vcol]
            q_in = (q * jnp.exp(bc)).astype(BF16)
            k_in = (k * jnp.exp(-bc)).astype(BF16)
            k_end = (k * jnp.exp(bl - bc)).astype(BF16)
            att = lax.dot_general(q_in, k_in, (((1,), (1,)), ((), ())), preferred_element_type=F32)
            att = jnp.where(causal, att, 0.0).astype(BF16)
            st = st_ref[h]
            o = jnp.dot(att, v, preferred_element_type=F32)
            o = o + lax.dot_general(q_in, st.astype(BF16), (((1,), (1,)), ((), ())),
                                    preferred_element_type=F32)
            st_ref[h] = st * jnp.exp(bl) + lax.dot_general(
                v, k_end, (((0,), (0,)), ((), ())), preferred_element_type=F32)
            if finalize:
                o = o + of_ref[sl, vcol].astype(F32)
                o = o * lax.rsqrt(jnp.mean(o * o, axis=-1, keepdims=True) + NORM_EPS) * gn_ref[...]
                r = r_ref[sl, vcol].astype(F32)
                o = o * (r * _sigmoid(r))
            o_ref[sl, vcol] = o.astype(o_ref.dtype)
        return carry

    lax.fori_loop(0, nchunks, body, 0, unroll=2)


def _gla_pass(z, a, w_a2, b_a, B, S, D, *, reverse, o_fwd=None, g_norm=None, tb=512):
    T = z.shape[0]
    H = GLA_HEADS
    DK = D // 2
    tb = _tile(S, tb)
    nb = S // tb
    finalize = o_fwd is not None
    d = 1 if reverse else 0

    def rb(b, i):
        return b * nb + ((nb - 1 - i) if reverse else i)

    in_specs = [pl.BlockSpec((tb, DK), lambda b, i: (rb(b, i), 0)),
                pl.BlockSpec((tb, DK), lambda b, i: (rb(b, i), 1)),
                pl.BlockSpec((tb, D), lambda b, i: (rb(b, i), 1)),
                pl.BlockSpec((tb, 2 * GLA_GATE_RANK), lambda b, i: (rb(b, i), 0)),
                pl.BlockSpec((None, GLA_GATE_RANK, DK), lambda b, i: (d, 0, 0)),
                pl.BlockSpec((None, 1, DK), lambda b, i: (d, 0, 0))]
    args = [z, z, z, a, w_a2, b_a.reshape(2, 1, DK)]
    if finalize:
        in_specs += [pl.BlockSpec((tb, D), lambda b, i: (rb(b, i), 0)),
                     pl.BlockSpec((tb, D), lambda b, i: (rb(b, i), 2)),
                     pl.BlockSpec((1, D // H), lambda b, i: (0, 0))]
        args += [o_fwd, z, g_norm.reshape(1, D // H)]
    return pl.pallas_call(
        functools.partial(_gla_kernel, reverse=reverse, finalize=finalize, nchunks=tb // GLA_CHUNK),
        grid=(B, nb),
        in_specs=in_specs,
        out_specs=pl.BlockSpec((tb, D), lambda b, i: (rb(b, i), 0)),
        out_shape=jax.ShapeDtypeStruct((T, D), BF16),
        scratch_shapes=[pltpu.VMEM((H, D // H, DK // H), F32)],
        compiler_params=_params(("parallel", "arbitrary")),
        name="gla_bwd_finalize" if finalize else "gla_fwd",
    )(*args)


def gla_mixer_core(z, a, w_a2, b_a, g_norm, B, S, D):
    o_f = _gla_pass(z, a, w_a2, b_a, B, S, D, reverse=False)
    return _gla_pass(z, a, w_a2, b_a, B, S, D, reverse=True, o_fwd=o_f, g_norm=g_norm)


def _pack_bf16_pairs(x):
    m = x.shape[1] // 2
    xb = x.astype(BF16).astype(F32)
    lo = pltpu.bitcast(xb[:, :m], U32) >> 16
    hi = pltpu.bitcast(xb[:, m:], U32)
    return hi | lo


def _unpack_bf16_pairs(p):
    lo = pltpu.bitcast(p << 16, F32)
    hi = pltpu.bitcast(p & jnp.uint32(0xFFFF0000), F32)
    return lo, hi


def _router_kernel(x_ref, g_ref, w_ref, b_ref, hp_ref, ri_ref, rw_ref, cnt_ref):
    h = _rms(x_ref[...], g_ref[...])
    hp_ref[...] = _pack_bf16_pairs(h)
    h_hi = h.astype(BF16)
    h_lo = (h - h_hi.astype(F32)).astype(BF16)
    parts = (jnp.dot(h_hi, w_ref[...], preferred_element_type=F32)
             + jnp.dot(h_lo, w_ref[...], preferred_element_type=F32))
    logits = parts[:, :LANES] + parts[:, LANES:] + b_ref[...]
    lane = lax.broadcasted_iota(I32, logits.shape, 1)
    lane_f = lane.astype(F32)
    neg = -jnp.inf

    def first_lane(hit):
        return jnp.min(jnp.where(hit, lane_f, float(LANES)), axis=-1, keepdims=True).astype(I32)

    gl = jnp.where(lane < MOE_GROUPS, logits, neg)
    gm = jnp.max(gl, axis=-1, keepdims=True)
    g_top = 1.0 / jnp.sum(jnp.exp(gl - gm), axis=-1, keepdims=True)
    g_idx = first_lane(gl == gm)
    e_lane = lane - MOE_GROUPS
    in_grp = (e_lane >= 0) & (e_lane < MOE_EXPERTS) & ((e_lane >> 3) == g_idx)
    el = jnp.where(in_grp, logits, neg)
    m1 = jnp.max(el, axis=-1, keepdims=True)
    es = jnp.sum(jnp.exp(el - m1), axis=-1, keepdims=True)
    i1 = first_lane(el == m1)
    el2 = jnp.where(lane == i1, neg, el)
    m2 = jnp.max(el2, axis=-1, keepdims=True)
    i2 = first_lane(el2 == m2)
    p1 = 1.0 / es
    p2 = jnp.exp(m2 - m1) / es
    w1 = g_top * p1 / (p1 + p2)
    w2 = g_top * p2 / (p1 + p2)
    e1 = i1 - MOE_GROUPS
    e2 = i2 - MOE_GROUPS
    tm = logits.shape[0]
    onehot = jnp.where((lane == e1) | (lane == e2), 1.0, 0.0)
    before = (lax.broadcasted_iota(I32, (tm, tm), 0) > lax.broadcasted_iota(I32, (tm, tm), 1))
    rank = jnp.dot(before.astype(BF16), onehot.astype(BF16), preferred_element_type=F32)
    r1 = jnp.sum(jnp.where(lane == e1, rank, 0.0), axis=-1, keepdims=True).astype(I32)
    r2 = jnp.sum(jnp.where(lane == e2, rank, 0.0), axis=-1, keepdims=True).astype(I32)
    ri_ref[...] = jnp.where(lane == 0, e1, jnp.where(lane == 1, e2,
                            jnp.where(lane == 2, r1, jnp.where(lane == 3, r2, 0))))
    rw_ref[...] = jnp.where(lane == 0, w1, jnp.where(lane == 1, w2, 0.0))
    cnt = jnp.sum(onehot, axis=0, keepdims=True).astype(I32)
    cnt_ref[...] = jnp.broadcast_to(cnt, cnt_ref.shape)


def moe_router(x, gain, w_router, b_router, tm=ROUTER_TILE):
    T, D = x.shape
    tm = _tile(T, tm)
    return pl.pallas_call(
        _router_kernel,
        grid=(T // tm,),
        in_specs=[pl.BlockSpec((tm, D), lambda i: (i, 0)),
                  pl.BlockSpec((1, D), lambda i: (0, 0)),
                  pl.BlockSpec((D, 2 * LANES), lambda i: (0, 0)),
                  pl.BlockSpec((1, LANES), lambda i: (0, 0))],
        out_specs=[pl.BlockSpec((tm, D // 2), lambda i: (i, 0)),
                   pl.BlockSpec((tm, LANES), lambda i: (i, 0)),
                   pl.BlockSpec((tm, LANES), lambda i: (i, 0)),
                   pl.BlockSpec((None, 8, LANES), lambda i: (i, 0, 0))],
        out_shape=[jax.ShapeDtypeStruct((T, D // 2), U32),
                   jax.ShapeDtypeStruct((T, LANES), I32),
                   jax.ShapeDtypeStruct((T, LANES), F32),
                   jax.ShapeDtypeStruct((T // tm, 8, LANES), I32)],
        compiler_params=_params(("parallel",)),
        name="moe_router",
    )(x, gain.reshape(1, D), w_router, b_router)


def _expert_kernel(be_ref, bv_ref, x_ref, wg_ref, wu_ref, wd_ref, o_ref):
    i = pl.program_id(0)
    nvalid = bv_ref[i]
    bm, dh = x_ref.shape
    half = bm // 2

    def run(rows):
        row = lax.broadcasted_iota(I32, (rows, dh), 0)
        xp = jnp.where(row < nvalid, x_ref[pl.ds(0, rows), :], jnp.uint32(0))
        lo, hi = _unpack_bf16_pairs(xp)
        x = jnp.concatenate([lo.astype(BF16), hi.astype(BF16)], axis=1)
        a = jnp.dot(x, wg_ref[...], preferred_element_type=F32)
        u = jnp.dot(x, wu_ref[...], preferred_element_type=F32)
        hmid = (a * _sigmoid(a) * u).astype(BF16)
        y = jnp.dot(hmid, wd_ref[...], preferred_element_type=F32)
        o_ref[pl.ds(0, rows), :] = _pack_bf16_pairs(y)
        if rows < bm:
            o_ref[pl.ds(rows, bm - rows), :] = jnp.zeros((bm - rows, dh), o_ref.dtype)

    @pl.when(nvalid > half)
    def _():
        run(bm)

    @pl.when((nvalid > 0) & (nvalid <= half))
    def _():
        run(half)

    @pl.when(nvalid == 0)
    def _():
        o_ref[...] = jnp.zeros_like(o_ref)


def moe_experts(xs, blk_e, blk_valid, w_gate, w_up, w_down, layer):
    n_rows, dh = xs.shape
    _, E, D, FF = w_gate.shape
    bm = MOE_ROW_BLOCK
    nblk = n_rows // bm
    grid_spec = pltpu.PrefetchScalarGridSpec(
        num_scalar_prefetch=2,
        grid=(nblk,),
        in_specs=[pl.BlockSpec((bm, dh), lambda i, be, bv: (i, 0)),
                  pl.BlockSpec((None, None, D, FF), lambda i, be, bv: (layer, be[i], 0, 0)),
                  pl.BlockSpec((None, None, D, FF), lambda i, be, bv: (layer, be[i], 0, 0)),
                  pl.BlockSpec((None, None, FF, D), lambda i, be, bv: (layer, be[i], 0, 0))],
        out_specs=pl.BlockSpec((bm, dh), lambda i, be, bv: (i, 0)),
    )
    return pl.pallas_call(
        _expert_kernel,
        grid_spec=grid_spec,
        out_shape=jax.ShapeDtypeStruct((n_rows, dh), U32),
        compiler_params=_params(("arbitrary",)),
        name="moe_experts",
    )(blk_e, blk_valid, xs, w_gate, w_up, w_down)


def _sc_mesh():
    return plsc.VectorSubcoreMesh(core_axis_name="c", subcore_axis_name="s")


def _sc_worker_id():
    return lax.axis_index("s") * SC_CORES + lax.axis_index("c")


def sc_scatter_rows(src, pos, n_rows):
    T, W = src.shape
    per_w = T // SC_WORKERS
    nch = per_w // SC_CHUNK
    assert per_w * SC_WORKERS == T and nch * SC_CHUNK == per_w
    idx = pos.reshape(SC_WORKERS, nch, SC_CHUNK, 2).transpose(0, 1, 3, 2).reshape(SC_WORKERS, 2 * nch, SC_CHUNK)

    @functools.partial(
        pl.kernel, mesh=_sc_mesh(),
        out_type=jax.ShapeDtypeStruct((n_rows, W), src.dtype),
        scratch_types=[pltpu.VMEM((2 * nch, SC_CHUNK), I32),
                       pltpu.VMEM((SC_CHUNK, W), src.dtype),
                       pltpu.SemaphoreType.DMA],
    )
    def k(src_hbm, idx_hbm, out_hbm, idx_v, rows_v, sem):
        wid = _sc_worker_id()
        base = wid * per_w
        pltpu.sync_copy(idx_hbm.at[wid], idx_v)

        @pl.loop(0, nch)
        def _(c):
            pltpu.sync_copy(src_hbm.at[pl.ds(base + c * SC_CHUNK, SC_CHUNK)], rows_v)
            pltpu.async_copy(rows_v, out_hbm.at[idx_v.at[2 * c]], sem).wait()
            pltpu.async_copy(rows_v, out_hbm.at[idx_v.at[2 * c + 1]], sem).wait()

    return k(src, idx)


def sc_gather_rows(table, idx_flat):
    N = idx_flat.shape[0]
    W = table.shape[1]
    per_w = N // SC_WORKERS
    nch = per_w // SC_CHUNK
    assert per_w * SC_WORKERS == N and nch * SC_CHUNK == per_w
    idx = idx_flat.reshape(SC_WORKERS, nch, SC_CHUNK)

    @functools.partial(
        pl.kernel, mesh=_sc_mesh(),
        out_type=jax.ShapeDtypeStruct((N, W), table.dtype),
        scratch_types=[pltpu.VMEM((nch, SC_CHUNK), I32),
                       pltpu.VMEM((SC_CHUNK, W), table.dtype),
                       pltpu.SemaphoreType.DMA],
    )
    def k(table_hbm, idx_hbm, out_hbm, idx_v, rows_v, sem):
        wid = _sc_worker_id()
        base = wid * per_w
        pltpu.sync_copy(idx_hbm.at[wid], idx_v)

        @pl.loop(0, nch)
        def _(c):
            pltpu.async_copy(table_hbm.at[idx_v.at[c]], rows_v, sem).wait()
            pltpu.sync_copy(rows_v, out_hbm.at[pl.ds(base + c * SC_CHUNK, SC_CHUNK)])

    return k(table, idx)


def _dispatch_plan(ri, tile_counts, bm):
    T = ri.shape[0]
    E = MOE_EXPERTS
    ntiles = tile_counts.shape[0]
    n_rows = (T * 2 // bm + E) * bm
    nblk = n_rows // bm
    eidx = ri[:, 0:2]
    tile_end = jnp.cumsum(tile_counts, axis=0)
    counts = tile_end[-1]
    padded = (counts + bm - 1) // bm * bm
    pad_end = jnp.cumsum(padded)
    base = pad_end - padded
    start = base[None, :] + tile_end - tile_counts
    experts = jnp.arange(E, dtype=I32)
    sel = eidx.reshape(ntiles, T // ntiles, 2, 1) == experts
    pos = jnp.sum(jnp.where(sel, start[:, None, None, :], 0), axis=-1).reshape(T, 2) + ri[:, 2:4]
    blk_start = jnp.arange(nblk, dtype=I32) * bm
    blk_e = jnp.minimum(jnp.sum((pad_end[None, :] <= blk_start[:, None]).astype(I32), axis=1), E - 1)
    cnt_end = jnp.sum(jnp.where(blk_e[:, None] == experts, (base + counts)[None, :], 0), axis=1)
    blk_valid = jnp.clip(cnt_end - blk_start, 0, bm).astype(I32)
    return pos.astype(I32), blk_e.astype(I32), blk_valid, n_rows


def _ple_kernel(x_ref, y_ref, rw_ref, p_ref, g_ref, wg_ref, wp_ref, *rest, sub, tn):
    o_ref = rest[-1]
    tm, D = x_ref.shape
    g = g_ref[...]
    for s0 in range(0, tm, sub):
        rows = pl.ds(s0, sub)
        rw = rw_ref[rows, :]
        lo0, hi0 = _unpack_bf16_pairs(y_ref[0, rows, :])
        lo1, hi1 = _unpack_bf16_pairs(y_ref[1, rows, :])
        w0 = rw[:, 0:1]
        w1 = rw[:, 1:2]
        y = jnp.concatenate([lo0 * w0 + lo1 * w1, hi0 * w0 + hi1 * w1], axis=1)
        x2 = x_ref[rows, :] + y
        hp = _rms(x2, g).astype(BF16)
        pb = p_ref[rows, :].astype(BF16)
        for c0 in range(0, D, tn):
            gate = _sigmoid(jnp.dot(hp, wg_ref[:, c0:c0 + tn], preferred_element_type=F32))
            proj = jnp.dot(pb, wp_ref[:, c0:c0 + tn], preferred_element_type=F32)
            o_ref[rows, c0:c0 + tn] = x2[:, c0:c0 + tn] + gate * proj
        if len(rest) == 2:
            o_ref[rows, :] = _rms(o_ref[rows, :], rest[0][...])


def moe_combine_ple(x, ysel, rw, p, gain, w_gate, w_proj, layer, final_gain=None,
                    tm=512, sub=256, tn=512):
    T, D = x.shape
    P = p.shape[2]
    tm, tn = _tile(T, tm), _tile(D, tn)
    sub = _tile(tm, sub)
    in_specs = [pl.BlockSpec((tm, D), lambda i: (i, 0)),
                pl.BlockSpec((2, tm, D // 2), lambda i: (0, i, 0)),
                pl.BlockSpec((tm, LANES), lambda i: (i, 0)),
                pl.BlockSpec((None, tm, P), lambda i: (layer, i, 0)),
                pl.BlockSpec((1, D), lambda i: (0, 0)),
                pl.BlockSpec((None, D, D), lambda i: (layer, 0, 0)),
                pl.BlockSpec((None, P, D), lambda i: (layer, 0, 0))]
    args = [x, ysel, rw, p, gain.reshape(1, D), w_gate, w_proj]
    if final_gain is not None:
        in_specs.append(pl.BlockSpec((1, D), lambda i: (0, 0)))
        args.append(final_gain.reshape(1, D))
    return pl.pallas_call(
        functools.partial(_ple_kernel, sub=sub, tn=tn),
        grid=(T // tm,),
        in_specs=in_specs,
        out_specs=pl.BlockSpec((tm, D), lambda i: (i, 0)),
        out_shape=jax.ShapeDtypeStruct((T, D), F32),
        compiler_params=_params(("parallel",)),
        name="moe_combine_ple",
    )(*args)


def _moe_layer(x, p, w, shared, layer, final_gain=None):
    T, D = x.shape
    hp, ri, rw, cnt = moe_router(x, w["norm_ffn"], w["w_router"], w["b_router"])
    pos, blk_e, blk_valid, n_rows = _dispatch_plan(ri, cnt[:, 0, :MOE_EXPERTS], MOE_ROW_BLOCK)
    xs = sc_scatter_rows(hp, pos, n_rows)
    ys = moe_experts(xs, blk_e, blk_valid, shared["moe_w_gate"], shared["moe_w_up"],
                     shared["moe_w_down"], layer)
    ysel = sc_gather_rows(ys, pos.T.reshape(-1)).reshape(2, T, D // 2)
    return moe_combine_ple(x, ysel, rw, p, w["norm_ple"], shared["ple_w_gate"],
                           shared["ple_w_proj"], layer, final_gain)


def _trunk(x, p, layers, shared, norm_final):
    B, S, D = x.shape
    x = x.reshape(B * S, D)
    p = p.reshape(p.shape[0], B * S, p.shape[-1])
    for i, w in enumerate(layers):
        kind = w["kind"]
        if kind == 0:
            qkv = norm_matmul(x, w["norm_mix"], w["w_qkv"], w["qkv_colscale"])
            o = na_attention(qkv, w["na_bias"], B, S)
            x = matmul_residual(o, w["w_o"], x)
        elif kind == 1:
            x = pool_mixer(x, w["norm_mix"], w["pool_w"], w["pool_scale"], B, S)
        else:
            z = norm_matmul(x, w["norm_mix"], w["w_in"], w["in_colscale"], tn=w["in_tn"])
            a = z[:, 3 * D:3 * D + 2 * GLA_GATE_RANK]
            o = gla_mixer_core(z, a, w["w_a2"], w["b_a"], w["g_norm"], B, S, D)
            x = matmul_residual(o, w["w_o"], x)
        last = i == len(layers) - 1
        x = _moe_layer(x, p, w, shared, i, norm_final if last else None)
    return x.reshape(B, S, D)


def _prepare_layers(norm_mix, norm_ffn, norm_ple, na_w_qkv, na_rpb, na_w_o, pool_w, pool_scale,
                    gla_w_in, gla_w_a2, gla_b_a, gla_norm, gla_w_o, moe_w_rg, moe_b_rg, moe_w_re,
                    moe_b_re, moe_w_gate, moe_w_up, moe_w_down, ple_w_proj, ple_w_gate):
    depth, D = norm_mix.shape
    layers = []
    for i in range(depth):
        j, kind = i // 3, i % 3
        w = {"kind": kind, "norm_mix": norm_mix[i], "norm_ffn": norm_ffn[i], "norm_ple": norm_ple[i]}
        if kind == 0:
            w["w_qkv"] = na_w_qkv[j].astype(BF16)
            w["qkv_colscale"] = jnp.concatenate(
                [jnp.full((D,), NA_HEAD_DIM ** -0.5 * LOG2E, F32), jnp.ones((2 * D,), F32)])
            w["na_bias"] = _na_bias_table(na_rpb[j])
            w["w_o"] = na_w_o[j].astype(BF16)
        elif kind == 1:
            w["pool_w"] = pool_w[j].astype(BF16)
            w["pool_scale"] = pool_scale[j]
        else:
            n_in = gla_w_in.shape[2]
            n_pad = -(-n_in // (7 * LANES)) * (7 * LANES)
            w["w_in"] = jnp.pad(gla_w_in[j], ((0, 0), (0, n_pad - n_in))).astype(BF16)
            dkh = D // 2 // GLA_HEADS
            w["in_colscale"] = jnp.concatenate(
                [jnp.full((D // 2,), dkh ** -0.5, F32), jnp.ones((n_pad - D // 2,), F32)])
            w["in_tn"] = n_pad // 7
            w["w_a2"] = gla_w_a2[j]
            w["b_a"] = gla_b_a[j]
            w["g_norm"] = gla_norm[j]
            w["w_o"] = gla_w_o[j].astype(BF16)
        wr = jnp.concatenate([moe_w_rg[i], moe_w_re[i]], axis=1)
        br = jnp.concatenate([moe_b_rg[i], moe_b_re[i]])
        wr = jnp.pad(wr, ((0, 0), (0, LANES - wr.shape[1])))
        wr_hi = wr.astype(BF16)
        wr_lo = (wr - wr_hi.astype(F32)).astype(BF16)
        w["w_router"] = jnp.concatenate([wr_hi, wr_lo], axis=1)
        w["b_router"] = jnp.pad(br, (0, LANES - br.shape[0])).reshape(1, LANES)
        layers.append(w)
    shared = {"moe_w_gate": moe_w_gate.astype(BF16), "moe_w_up": moe_w_up.astype(BF16),
              "moe_w_down": moe_w_down.astype(BF16), "ple_w_gate": ple_w_gate.astype(BF16),
              "ple_w_proj": ple_w_proj.astype(BF16)}
    return layers, shared


def kernel(x_prompt, x_sample, p_prompt, p_sample, norm_mix, norm_ffn, norm_ple, norm_final, na_w_qkv, na_rpb, na_w_o, pool_w, pool_scale, gla_w_in, gla_w_a2, gla_b_a, gla_norm, gla_w_o, moe_w_rg, moe_b_rg, moe_w_re, moe_b_re, moe_w_gate, moe_w_up, moe_w_down, ple_w_proj, ple_w_gate):
    layers, shared = _prepare_layers(
        norm_mix, norm_ffn, norm_ple, na_w_qkv, na_rpb, na_w_o, pool_w, pool_scale, gla_w_in, gla_w_a2,
        gla_b_a, gla_norm, gla_w_o, moe_w_rg, moe_b_rg, moe_w_re, moe_b_re, moe_w_gate, moe_w_up,
        moe_w_down, ple_w_proj, ple_w_gate)
    y_prompt = _trunk(x_prompt, p_prompt, layers, shared, norm_final)
    y_sample = _trunk(x_sample, p_sample, layers, shared, norm_final)
    return (y_prompt, y_sample)
```

```python
import functools

import numpy as np
import jax
import jax.numpy as jnp
from jax import lax
from jax.experimental import pallas as pl
from jax.experimental.pallas import tpu as pltpu
from jax.experimental.pallas import tpu_sc as plsc

F32 = jnp.float32
BF16 = jnp.bfloat16
I32 = jnp.int32
U32 = jnp.uint32

NORM_EPS = 1e-6
LOG2E = 1.4426950408889634
GRID_W = 64
NA_HEAD_DIM = 32
NA_WIN_H = 8
NA_WIN_W = 16
NA_HEADS_PER_STEP = 4
NA_ROW_UNROLL = 32
POOL_WINDOWS = (2, 4, 8, 16)
POOL_HALO = 16
GLA_HEADS = 4
GLA_GATE_RANK = 16
GLA_TAU = 16.0
GLA_CHUNK = 64
MOE_GROUPS = 4
MOE_PER_GROUP = 8
MOE_EXPERTS = MOE_GROUPS * MOE_PER_GROUP
MOE_ROW_BLOCK = 512
ROUTER_TILE = 512
LANES = 128
SC_CORES = 2
SC_WORKERS = SC_CORES * 16
SC_CHUNK = 32
VMEM_LIMIT = 56 * 1024 * 1024


def _tile(n, pref):
    t = min(n, pref)
    assert n % t == 0, (n, pref)
    return t


def _params(sem, vmem=VMEM_LIMIT):
    return pltpu.CompilerParams(dimension_semantics=sem, vmem_limit_bytes=vmem)


def _rms(x, g):
    ms = jnp.mean(x * x, axis=-1, keepdims=True)
    return x * lax.rsqrt(ms + NORM_EPS) * g


def _sigmoid(x):
    return 0.5 * jnp.tanh(0.5 * x) + 0.5


def _norm_mm_kernel(x_ref, g_ref, w_ref, cs_ref, o_ref, h_ref):
    @pl.when(pl.program_id(1) == 0)
    def _():
        h_ref[...] = _rms(x_ref[...], g_ref[...]).astype(BF16)

    acc = jnp.dot(h_ref[...], w_ref[...], preferred_element_type=F32)
    o_ref[...] = (acc * cs_ref[...]).astype(o_ref.dtype)


def norm_matmul(x, gain, w, colscale, out_dtype=BF16, tm=1024, tn=1024):
    T, D = x.shape
    N = w.shape[1]
    tm, tn = _tile(T, tm), _tile(N, tn)
    return pl.pallas_call(
        _norm_mm_kernel,
        grid=(T // tm, N // tn),
        in_specs=[pl.BlockSpec((tm, D), lambda i, j: (i, 0)),
                  pl.BlockSpec((1, D), lambda i, j: (0, 0)),
                  pl.BlockSpec((D, tn), lambda i, j: (0, j)),
                  pl.BlockSpec((1, tn), lambda i, j: (0, j))],
        out_specs=pl.BlockSpec((tm, tn), lambda i, j: (i, j)),
        out_shape=jax.ShapeDtypeStruct((T, N), out_dtype),
        scratch_shapes=[pltpu.VMEM((tm, D), BF16)],
        compiler_params=_params(("parallel", "arbitrary")),
        name="norm_matmul",
    )(x, gain.reshape(1, D), w, colscale.reshape(1, N))


def _mm_res_kernel(a_ref, w_ref, r_ref, o_ref, *, tn):
    a = a_ref[...]
    for c0 in range(0, o_ref.shape[1], tn):
        o_ref[:, c0:c0 + tn] = r_ref[:, c0:c0 + tn] + jnp.dot(
            a, w_ref[:, c0:c0 + tn], preferred_element_type=F32)


def matmul_residual(a, w, res, tm=512, tn=512):
    T, K = a.shape
    N = w.shape[1]
    tm, tn = _tile(T, tm), _tile(N, tn)
    return pl.pallas_call(
        functools.partial(_mm_res_kernel, tn=tn),
        grid=(T // tm,),
        in_specs=[pl.BlockSpec((tm, K), lambda i: (i, 0)),
                  pl.BlockSpec((K, N), lambda i: (0, 0)),
                  pl.BlockSpec((tm, N), lambda i: (i, 0))],
        out_specs=pl.BlockSpec((tm, N), lambda i: (i, 0)),
        out_shape=jax.ShapeDtypeStruct((T, N), F32),
        compiler_params=_params(("parallel",)),
        name="matmul_residual",
    )(a, w, res)


def _na_bias_table(rpb):
    H, nrow, ncol = rpb.shape
    G = H // NA_HEADS_PER_STEP
    assert ncol == 2 * NA_WIN_W - 1 and nrow == 2 * NA_WIN_H - 1 and 2 * GRID_W == LANES
    lead = GRID_W - NA_WIN_W
    rows = jnp.pad(rpb.astype(F32) * LOG2E, ((0, 0), (0, 0), (lead, LANES - lead - ncol)))
    q = np.arange(GRID_W)[:, None]
    c = np.arange(GRID_W)[None, :]
    wstart = np.clip(q - NA_WIN_W // 2, 0, GRID_W - NA_WIN_W)
    ok = (c >= wstart) & (c < wstart + NA_WIN_W)
    mask = np.where(np.concatenate([ok, ok], axis=1), 0.0, -np.inf).astype(np.float32)
    nq = NA_HEADS_PER_STEP * GRID_W
    nkeys = NA_WIN_H * GRID_W
    return pl.pallas_call(
        _na_bias_kernel,
        grid=(G,),
        in_specs=[pl.BlockSpec((NA_HEADS_PER_STEP, nrow, LANES), lambda g: (g, 0, 0)),
                  pl.BlockSpec((GRID_W, LANES), lambda g: (0, 0))],
        out_specs=pl.BlockSpec((None, NA_WIN_H, nq, nkeys), lambda g: (g, 0, 0, 0)),
        out_shape=jax.ShapeDtypeStruct((G, NA_WIN_H, nq, nkeys), F32),
        compiler_params=_params(("parallel",)),
        name="na_bias_table",
    )(rows, jnp.asarray(mask))


def _na_bias_kernel(r_ref, m_ref, o_ref):
    lane = lax.broadcasted_iota(I32, (GRID_W, LANES), 1)
    low = lane < GRID_W
    mask = m_ref[...]
    for dl in range(NA_WIN_H):
        for h in range(NA_HEADS_PER_STEP):
            for kp in range(NA_WIN_H // 2):
                drow = 2 * kp - dl + NA_WIN_H - 1
                r_even = jnp.broadcast_to(r_ref[h, pl.ds(drow, 1), :], (GRID_W, LANES))
                r_odd = jnp.broadcast_to(r_ref[h, pl.ds(drow + 1, 1), :], (GRID_W, LANES))
                t_even = pltpu.roll(r_even, GRID_W + 1, 1, stride=1, stride_axis=0)
                t_odd = pltpu.roll(r_odd, 1, 1, stride=1, stride_axis=0)
                o_ref[dl, pl.ds(h * GRID_W, GRID_W), pl.ds(kp * LANES, LANES)] = (
                    jnp.where(low, t_even, t_odd) + mask)


def _na_kernel(q_ref, k_ref, v_ref, b_ref, o_ref, *, rows):
    nkeys = NA_WIN_H * GRID_W
    lane_head = lax.broadcasted_iota(I32, (GRID_W, LANES), 1) // NA_HEAD_DIM
    masks = [lane_head == h for h in range(NA_HEADS_PER_STEP)]

    def body(r, carry):
        rs = jnp.clip(r - NA_WIN_H // 2, 0, rows - NA_WIN_H)
        q = q_ref[pl.ds(pl.multiple_of(r * GRID_W, GRID_W), GRID_W), :]
        kk = k_ref[pl.ds(pl.multiple_of(rs * GRID_W, GRID_W), nkeys), :]
        vv = v_ref[pl.ds(pl.multiple_of(rs * GRID_W, GRID_W), nkeys), :]
        zero = jnp.zeros_like(q)
        qm = jnp.concatenate([jnp.where(m, q, zero) for m in masks], axis=0)
        s = lax.dot_general(qm, kk, (((1,), (1,)), ((), ())), preferred_element_type=F32)
        s = s + b_ref[r - rs]
        m = jnp.max(s, axis=-1, keepdims=True)
        p = jnp.exp2((s - m).astype(BF16))
        v_ext = jnp.concatenate([vv, jnp.ones_like(vv)], axis=1)
        pv = jnp.dot(p, v_ext, preferred_element_type=F32)
        pv = pv[:, :LANES] / pv[:, LANES:]
        o = jnp.zeros((GRID_W, LANES), F32)
        for h in range(NA_HEADS_PER_STEP):
            o = jnp.where(masks[h], pv[h * GRID_W:(h + 1) * GRID_W], o)
        o_ref[pl.ds(pl.multiple_of(r * GRID_W, GRID_W), GRID_W), :] = o.astype(o_ref.dtype)
        return carry

    lax.fori_loop(0, rows, body, 0, unroll=NA_ROW_UNROLL)


def na_attention(qkv, bias, B, S):
    T, D3 = qkv.shape
    D = D3 // 3
    G = D // LANES
    rows = S // GRID_W
    assert rows >= NA_WIN_H and S % GRID_W == 0
    nq = NA_HEADS_PER_STEP * GRID_W
    nkeys = NA_WIN_H * GRID_W
    return pl.pallas_call(
        functools.partial(_na_kernel, rows=rows),
        grid=(G, B),
        in_specs=[pl.BlockSpec((S, LANES), lambda g, b: (b, g)),
                  pl.BlockSpec((S, LANES), lambda g, b: (b, G + g)),
                  pl.BlockSpec((S, LANES), lambda g, b: (b, 2 * G + g)),
                  pl.BlockSpec((None, NA_WIN_H, nq, nkeys), lambda g, b: (g, 0, 0, 0))],
        out_specs=pl.BlockSpec((S, LANES), lambda g, b: (b, g)),
        out_shape=jax.ShapeDtypeStruct((T, D), BF16),
        compiler_params=_params(("parallel", "parallel")),
        name="na_attention",
    )(qkv, qkv, qkv, bias)


def _pool_kernel(xp_ref, xc_ref, xn_ref, g_ref, w_ref, sc_ref, o_ref, h_ref, *, S, tm):
    i = pl.program_id(1)
    nt = pl.num_programs(1)
    g = g_ref[...]
    D = xc_ref.shape[1]
    dg = D // len(POOL_WINDOWS)
    xc = xc_ref[...]
    hc = _rms(xc, g)
    h_ref[pl.ds(0, POOL_HALO), :] = jnp.where(i > 0, _rms(xp_ref[...], g), 0.0)
    h_ref[pl.ds(POOL_HALO, tm), :] = hc
    h_ref[pl.ds(POOL_HALO + tm, POOL_HALO), :] = jnp.where(i < nt - 1, _rms(xn_ref[...], g), 0.0)
    t = i * tm + lax.broadcasted_iota(I32, (tm, 1), 0)
    for gi, win in enumerate(POOL_WINDOWS):
        c0 = gi * dg
        acc = h_ref[pl.ds(POOL_HALO - win // 2, tm), pl.ds(c0, dg)]
        for o in range(-win // 2 + 1, win // 2):
            acc = acc + h_ref[pl.ds(POOL_HALO + o, tm), pl.ds(c0, dg)]
        cnt = jnp.minimum(t + win // 2, S) - jnp.maximum(t - win // 2, 0)
        d = acc / cnt.astype(F32) - hc[:, c0:c0 + dg]
        y = jnp.dot(d.astype(BF16), w_ref[gi], preferred_element_type=F32)
        o_ref[:, pl.ds(c0, dg)] = xc[:, c0:c0 + dg] + y * sc_ref[:, pl.ds(c0, dg)]


def pool_mixer(x, gain, w_pool, scale, B, S, tm=256):
    T, D = x.shape
    tm = _tile(S, tm)
    nt = S // tm
    hb = tm // POOL_HALO
    nhb = S // POOL_HALO
    return pl.pallas_call(
        functools.partial(_pool_kernel, S=S, tm=tm),
        grid=(B, nt),
        in_specs=[pl.BlockSpec((POOL_HALO, D), lambda b, i: (b * nhb + jnp.maximum(i * hb - 1, 0), 0)),
                  pl.BlockSpec((tm, D), lambda b, i: (b * nt + i, 0)),
                  pl.BlockSpec((POOL_HALO, D),
                               lambda b, i: (b * nhb + jnp.minimum((i + 1) * hb, nhb - 1), 0)),
                  pl.BlockSpec((1, D), lambda b, i: (0, 0)),
                  pl.BlockSpec(w_pool.shape, lambda b, i: (0, 0, 0)),
                  pl.BlockSpec((1, D), lambda b, i: (0, 0))],
        out_specs=pl.BlockSpec((tm, D), lambda b, i: (b * nt + i, 0)),
        out_shape=jax.ShapeDtypeStruct((T, D), F32),
        scratch_shapes=[pltpu.VMEM((tm + 2 * POOL_HALO, D), F32)],
        compiler_params=_params(("parallel", "arbitrary")),
        name="pool_mixer",
    )(x, x, x, gain.reshape(1, D), w_pool, scale.reshape(1, D))


def _gla_kernel(*refs, reverse, finalize, nchunks):
    if finalize:
        (q_ref, k_ref, v_ref, a_ref, wa_ref, ba_ref, of_ref, r_ref, gn_ref, o_ref, st_ref) = refs
    else:
        (q_ref, k_ref, v_ref, a_ref, wa_ref, ba_ref, o_ref, st_ref) = refs
    C = GLA_CHUNK
    H = GLA_HEADS
    dk = q_ref.shape[1] // H
    dv = v_ref.shape[1] // H

    @pl.when(pl.program_id(1) == 0)
    def _():
        st_ref[...] = jnp.zeros_like(st_ref)

    ii = lax.broadcasted_iota(I32, (C, C), 0)
    jj = lax.broadcasted_iota(I32, (C, C), 1)
    causal = (ii <= jj) if reverse else (ii >= jj)
    tri = causal.astype(F32)
    a_lo = GLA_GATE_RANK if reverse else 0
    wa = wa_ref[...]
    ba = ba_ref[...]

    def body(ci, carry):
        c = (nchunks - 1 - ci) if reverse else ci
        sl = pl.ds(pl.multiple_of(c * C, C), C)
        a = a_ref[sl, :][:, a_lo:a_lo + GLA_GATE_RANK].astype(F32)
        logit = jnp.dot(a, wa, precision=lax.Precision.HIGHEST, preferred_element_type=F32) + ba
        log_sig = jnp.minimum(logit, 0.0) - jnp.log1p(jnp.exp(-jnp.abs(logit)))
        la = log_sig * (1.0 / GLA_TAU)
        bc_all = jnp.dot(tri, la, precision=lax.Precision.HIGHEST, preferred_element_type=F32)
        bl_all = jnp.sum(la, axis=0, keepdims=True)
        for h in range(H):
            kcol = slice(h * dk, (h + 1) * dk)
            vcol = slice(h * dv, (h + 1) * dv)
            bc = bc_all[:, kcol]
            bl = bl_all[:, kcol]
            q = q_ref[sl, kcol].astype(F32)
            k = k_ref[sl, kcol].astype(F32)
            v = v_ref[sl, vcol]
            q_in = (q * jnp.exp(bc)).astype(BF16)
            k_in = (k * jnp.exp(-bc)).astype(BF16)
            k_end = (k * jnp.exp(bl - bc)).astype(BF16)
            att = lax.dot_general(q_in, k_in, (((1,), (1,)), ((), ())), preferred_element_type=F32)
            att = jnp.where(causal, att, 0.0).astype(BF16)
            st = st_ref[h]
            o = jnp.dot(att, v, preferred_element_type=F32)
            o = o + lax.dot_general(q_in, st.astype(BF16), (((1,), (1,)), ((), ())),
                                    preferred_element_type=F32)
            st_ref[h] = st * jnp.exp(bl) + lax.dot_general(
                v, k_end, (((0,), (0,)), ((), ())), preferred_element_type=F32)
            if finalize:
                o = o + of_ref[sl, vcol].astype(F32)
                o = o * lax.rsqrt(jnp.mean(o * o, axis=-1, keepdims=True) + NORM_EPS) * gn_ref[...]
                r = r_ref[sl, vcol].astype(F32)
                o = o * (r * _sigmoid(r))
            o_ref[sl, vcol] = o.astype(o_ref.dtype)
        return carry

    lax.fori_loop(0, nchunks, body, 0, unroll=4)


def _gla_pass(z, a, w_a2, b_a, B, S, D, *, reverse, o_fwd=None, g_norm=None, tb=512):
    T = z.shape[0]
    H = GLA_HEADS
    DK = D // 2
    tb = _tile(S, tb)
    nb = S // tb
    finalize = o_fwd is not None
    d = 1 if reverse else 0

    def rb(b, i):
        return b * nb + ((nb - 1 - i) if reverse else i)

    in_specs = [pl.BlockSpec((tb, DK), lambda b, i: (rb(b, i), 0)),
                pl.BlockSpec((tb, DK), lambda b, i: (rb(b, i), 1)),
                pl.BlockSpec((tb, D), lambda b, i: (rb(b, i), 1)),
                pl.BlockSpec((tb, 2 * GLA_GATE_RANK), lambda b, i: (rb(b, i), 0)),
                pl.BlockSpec((None, GLA_GATE_RANK, DK), lambda b, i: (d, 0, 0)),
                pl.BlockSpec((None, 1, DK), lambda b, i: (d, 0, 0))]
    args = [z, z, z, a, w_a2, b_a.reshape(2, 1, DK)]
    if finalize:
        in_specs += [pl.BlockSpec((tb, D), lambda b, i: (rb(b, i), 0)),
                     pl.BlockSpec((tb, D), lambda b, i: (rb(b, i), 2)),
                     pl.BlockSpec((1, D // H), lambda b, i: (0, 0))]
        args += [o_fwd, z, g_norm.reshape(1, D // H)]
    return pl.pallas_call(
        functools.partial(_gla_kernel, reverse=reverse, finalize=finalize, nchunks=tb // GLA_CHUNK),
        grid=(B, nb),
        in_specs=in_specs,
        out_specs=pl.BlockSpec((tb, D), lambda b, i: (rb(b, i), 0)),
        out_shape=jax.ShapeDtypeStruct((T, D), BF16),
        scratch_shapes=[pltpu.VMEM((H, D // H, DK // H), F32)],
        compiler_params=_params(("parallel", "arbitrary")),
        name="gla_bwd_finalize" if finalize else "gla_fwd",
    )(*args)


def gla_mixer_core(z, a, w_a2, b_a, g_norm, B, S, D):
    o_f = _gla_pass(z, a, w_a2, b_a, B, S, D, reverse=False)
    return _gla_pass(z, a, w_a2, b_a, B, S, D, reverse=True, o_fwd=o_f, g_norm=g_norm)


def _pack_bf16_pairs(x):
    m = x.shape[1] // 2
    xb = x.astype(BF16).astype(F32)
    lo = pltpu.bitcast(xb[:, :m], U32) >> 16
    hi = pltpu.bitcast(xb[:, m:], U32)
    return hi | lo


def _unpack_bf16_pairs(p):
    lo = pltpu.bitcast(p << 16, F32)
    hi = pltpu.bitcast(p & jnp.uint32(0xFFFF0000), F32)
    return lo, hi


def _router_kernel(x_ref, g_ref, w_ref, b_ref, hp_ref, ri_ref, rw_ref, cnt_ref):
    h = _rms(x_ref[...], g_ref[...])
    hp_ref[...] = _pack_bf16_pairs(h)
    h_hi = h.astype(BF16)
    h_lo = (h - h_hi.astype(F32)).astype(BF16)
    parts = (jnp.dot(h_hi, w_ref[...], preferred_element_type=F32)
             + jnp.dot(h_lo, w_ref[...], preferred_element_type=F32))
    logits = parts[:, :LANES] + parts[:, LANES:] + b_ref[...]
    lane = lax.broadcasted_iota(I32, logits.shape, 1)
    lane_f = lane.astype(F32)
    neg = -jnp.inf

    def first_lane(hit):
        return jnp.min(jnp.where(hit, lane_f, float(LANES)), axis=-1, keepdims=True).astype(I32)

    gl = jnp.where(lane < MOE_GROUPS, logits, neg)
    gm = jnp.max(gl, axis=-1, keepdims=True)
    g_top = 1.0 / jnp.sum(jnp.exp(gl - gm), axis=-1, keepdims=True)
    g_idx = first_lane(gl == gm)
    e_lane = lane - MOE_GROUPS
    in_grp = (e_lane >= 0) & (e_lane < MOE_EXPERTS) & ((e_lane >> 3) == g_idx)
    el = jnp.where(in_grp, logits, neg)
    m1 = jnp.max(el, axis=-1, keepdims=True)
    es = jnp.sum(jnp.exp(el - m1), axis=-1, keepdims=True)
    i1 = first_lane(el == m1)
    el2 = jnp.where(lane == i1, neg, el)
    m2 = jnp.max(el2, axis=-1, keepdims=True)
    i2 = first_lane(el2 == m2)
    p1 = 1.0 / es
    p2 = jnp.exp(m2 - m1) / es
    w1 = g_top * p1 / (p1 + p2)
    w2 = g_top * p2 / (p1 + p2)
    e1 = i1 - MOE_GROUPS
    e2 = i2 - MOE_GROUPS
    tm = logits.shape[0]
    onehot = jnp.where((lane == e1) | (lane == e2), 1.0, 0.0)
    before = (lax.broadcasted_iota(I32, (tm, tm), 0) > lax.broadcasted_iota(I32, (tm, tm), 1))
    rank = jnp.dot(before.astype(BF16), onehot.astype(BF16), preferred_element_type=F32)
    r1 = jnp.sum(jnp.where(lane == e1, rank, 0.0), axis=-1, keepdims=True).astype(I32)
    r2 = jnp.sum(jnp.where(lane == e2, rank, 0.0), axis=-1, keepdims=True).astype(I32)
    ri_ref[...] = jnp.where(lane == 0, e1, jnp.where(lane == 1, e2,
                            jnp.where(lane == 2, r1, jnp.where(lane == 3, r2, 0))))
    rw_ref[...] = jnp.where(lane == 0, w1, jnp.where(lane == 1, w2, 0.0))
    cnt = jnp.sum(onehot, axis=0, keepdims=True).astype(I32)
    cnt_ref[...] = jnp.broadcast_to(cnt, cnt_ref.shape)


def moe_router(x, gain, w_router, b_router, tm=ROUTER_TILE):
    T, D = x.shape
    tm = _tile(T, tm)
    return pl.pallas_call(
        _router_kernel,
        grid=(T // tm,),
        in_specs=[pl.BlockSpec((tm, D), lambda i: (i, 0)),
                  pl.BlockSpec((1, D), lambda i: (0, 0)),
                  pl.BlockSpec((D, 2 * LANES), lambda i: (0, 0)),
                  pl.BlockSpec((1, LANES), lambda i: (0, 0))],
        out_specs=[pl.BlockSpec((tm, D // 2), lambda i: (i, 0)),
                   pl.BlockSpec((tm, LANES), lambda i: (i, 0)),
                   pl.BlockSpec((tm, LANES), lambda i: (i, 0)),
                   pl.BlockSpec((None, 8, LANES), lambda i: (i, 0, 0))],
        out_shape=[jax.ShapeDtypeStruct((T, D // 2), U32),
                   jax.ShapeDtypeStruct((T, LANES), I32),
                   jax.ShapeDtypeStruct((T, LANES), F32),
                   jax.ShapeDtypeStruct((T // tm, 8, LANES), I32)],
        compiler_params=_params(("parallel",)),
        name="moe_router",
    )(x, gain.reshape(1, D), w_router, b_router)


def _expert_kernel(be_ref, bv_ref, x_ref, wg_ref, wu_ref, wd_ref, o_ref):
    i = pl.program_id(0)
    nvalid = bv_ref[i]

    @pl.when(nvalid > 0)
    def _():
        row = lax.broadcasted_iota(I32, x_ref.shape, 0)
        xp = jnp.where(row < nvalid, x_ref[...], jnp.uint32(0))
        lo, hi = _unpack_bf16_pairs(xp)
        x = jnp.concatenate([lo.astype(BF16), hi.astype(BF16)], axis=1)
        a = jnp.dot(x, wg_ref[...], preferred_element_type=F32)
        u = jnp.dot(x, wu_ref[...], preferred_element_type=F32)
        hmid = (a * _sigmoid(a) * u).astype(BF16)
        y = jnp.dot(hmid, wd_ref[...], preferred_element_type=F32)
        o_ref[...] = _pack_bf16_pairs(y)

    @pl.when(nvalid == 0)
    def _():
        o_ref[...] = jnp.zeros_like(o_ref)


def moe_experts(xs, blk_e, blk_valid, w_gate, w_up, w_down, layer):
    n_rows, dh = xs.shape
    _, E, D, FF = w_gate.shape
    bm = MOE_ROW_BLOCK
    nblk = n_rows // bm
    grid_spec = pltpu.PrefetchScalarGridSpec(
        num_scalar_prefetch=2,
        grid=(nblk,),
        in_specs=[pl.BlockSpec((bm, dh), lambda i, be, bv: (i, 0)),
                  pl.BlockSpec((None, None, D, FF), lambda i, be, bv: (layer, be[i], 0, 0)),
                  pl.BlockSpec((None, None, D, FF), lambda i, be, bv: (layer, be[i], 0, 0)),
                  pl.BlockSpec((None, None, FF, D), lambda i, be, bv: (layer, be[i], 0, 0))],
        out_specs=pl.BlockSpec((bm, dh), lambda i, be, bv: (i, 0)),
    )
    return pl.pallas_call(
        _expert_kernel,
        grid_spec=grid_spec,
        out_shape=jax.ShapeDtypeStruct((n_rows, dh), U32),
        compiler_params=_params(("arbitrary",)),
        name="moe_experts",
    )(blk_e, blk_valid, xs, w_gate, w_up, w_down)


def _sc_mesh():
    return plsc.VectorSubcoreMesh(core_axis_name="c", subcore_axis_name="s")


def _sc_worker_id():
    return lax.axis_index("s") * SC_CORES + lax.axis_index("c")


def sc_scatter_rows(src, pos, n_rows):
    T, W = src.shape
    per_w = T // SC_WORKERS
    nch = per_w // SC_CHUNK
    assert per_w * SC_WORKERS == T and nch * SC_CHUNK == per_w
    idx = pos.reshape(SC_WORKERS, nch, SC_CHUNK, 2).transpose(0, 1, 3, 2).reshape(SC_WORKERS, 2 * nch, SC_CHUNK)

    @functools.partial(
        pl.kernel, mesh=_sc_mesh(),
        out_type=jax.ShapeDtypeStruct((n_rows, W), src.dtype),
        scratch_types=[pltpu.VMEM((2 * nch, SC_CHUNK), I32),
                       pltpu.VMEM((SC_CHUNK, W), src.dtype),
                       pltpu.SemaphoreType.DMA],
    )
    def k(src_hbm, idx_hbm, out_hbm, idx_v, rows_v, sem):
        wid = _sc_worker_id()
        base = wid * per_w
        pltpu.sync_copy(idx_hbm.at[wid], idx_v)

        @pl.loop(0, nch)
        def _(c):
            pltpu.sync_copy(src_hbm.at[pl.ds(base + c * SC_CHUNK, SC_CHUNK)], rows_v)
            pltpu.async_copy(rows_v, out_hbm.at[idx_v.at[2 * c]], sem).wait()
            pltpu.async_copy(rows_v, out_hbm.at[idx_v.at[2 * c + 1]], sem).wait()

    return k(src, idx)


def sc_gather_rows(table, idx_flat):
    N = idx_flat.shape[0]
    W = table.shape[1]
    per_w = N // SC_WORKERS
    nch = per_w // SC_CHUNK
    assert per_w * SC_WORKERS == N and nch * SC_CHUNK == per_w
    idx = idx_flat.reshape(SC_WORKERS, nch, SC_CHUNK)

    @functools.partial(
        pl.kernel, mesh=_sc_mesh(),
        out_type=jax.ShapeDtypeStruct((N, W), table.dtype),
        scratch_types=[pltpu.VMEM((nch, SC_CHUNK), I32),
                       pltpu.VMEM((SC_CHUNK, W), table.dtype),
                       pltpu.SemaphoreType.DMA],
    )
    def k(table_hbm, idx_hbm, out_hbm, idx_v, rows_v, sem):
        wid = _sc_worker_id()
        base = wid * per_w
        pltpu.sync_copy(idx_hbm.at[wid], idx_v)

        @pl.loop(0, nch)
        def _(c):
            pltpu.async_copy(table_hbm.at[idx_v.at[c]], rows_v, sem).wait()
            pltpu.sync_copy(rows_v, out_hbm.at[pl.ds(base + c * SC_CHUNK, SC_CHUNK)])

    return k(table, idx)


def _dispatch_plan(ri, tile_counts, bm):
    T = ri.shape[0]
    E = MOE_EXPERTS
    ntiles = tile_counts.shape[0]
    n_rows = (T * 2 // bm + E) * bm
    nblk = n_rows // bm
    eidx = ri[:, 0:2]
    tile_end = jnp.cumsum(tile_counts, axis=0)
    counts = tile_end[-1]
    padded = (counts + bm - 1) // bm * bm
    pad_end = jnp.cumsum(padded)
    base = pad_end - padded
    start = base[None, :] + tile_end - tile_counts
    experts = jnp.arange(E, dtype=I32)
    sel = eidx.reshape(ntiles, T // ntiles, 2, 1) == experts
    pos = jnp.sum(jnp.where(sel, start[:, None, None, :], 0), axis=-1).reshape(T, 2) + ri[:, 2:4]
    blk_start = jnp.arange(nblk, dtype=I32) * bm
    blk_e = jnp.minimum(jnp.sum((pad_end[None, :] <= blk_start[:, None]).astype(I32), axis=1), E - 1)
    cnt_end = jnp.sum(jnp.where(blk_e[:, None] == experts, (base + counts)[None, :], 0), axis=1)
    blk_valid = jnp.clip(cnt_end - blk_start, 0, bm).astype(I32)
    return pos.astype(I32), blk_e.astype(I32), blk_valid, n_rows


def _ple_kernel(x_ref, y_ref, rw_ref, p_ref, g_ref, wg_ref, wp_ref, *rest, sub, tn):
    o_ref = rest[-1]
    tm, D = x_ref.shape
    g = g_ref[...]
    for s0 in range(0, tm, sub):
        rows = pl.ds(s0, sub)
        rw = rw_ref[rows, :]
        lo0, hi0 = _unpack_bf16_pairs(y_ref[0, rows, :])
        lo1, hi1 = _unpack_bf16_pairs(y_ref[1, rows, :])
        w0 = rw[:, 0:1]
        w1 = rw[:, 1:2]
        y = jnp.concatenate([lo0 * w0 + lo1 * w1, hi0 * w0 + hi1 * w1], axis=1)
        x2 = x_ref[rows, :] + y
        hp = _rms(x2, g).astype(BF16)
        pb = p_ref[rows, :].astype(BF16)
        for c0 in range(0, D, tn):
            gate = _sigmoid(jnp.dot(hp, wg_ref[:, c0:c0 + tn], preferred_element_type=F32))
            proj = jnp.dot(pb, wp_ref[:, c0:c0 + tn], preferred_element_type=F32)
            o_ref[rows, c0:c0 + tn] = x2[:, c0:c0 + tn] + gate * proj
        if len(rest) == 2:
            o_ref[rows, :] = _rms(o_ref[rows, :], rest[0][...])


def moe_combine_ple(x, ysel, rw, p, gain, w_gate, w_proj, layer, final_gain=None,
                    tm=512, sub=256, tn=512):
    T, D = x.shape
    P = p.shape[2]
    tm, tn = _tile(T, tm), _tile(D, tn)
    sub = _tile(tm, sub)
    in_specs = [pl.BlockSpec((tm, D), lambda i: (i, 0)),
                pl.BlockSpec((2, tm, D // 2), lambda i: (0, i, 0)),
                pl.BlockSpec((tm, LANES), lambda i: (i, 0)),
                pl.BlockSpec((None, tm, P), lambda i: (layer, i, 0)),
                pl.BlockSpec((1, D), lambda i: (0, 0)),
                pl.BlockSpec((None, D, D), lambda i: (layer, 0, 0)),
                pl.BlockSpec((None, P, D), lambda i: (layer, 0, 0))]
    args = [x, ysel, rw, p, gain.reshape(1, D), w_gate, w_proj]
    if final_gain is not None:
        in_specs.append(pl.BlockSpec((1, D), lambda i: (0, 0)))
        args.append(final_gain.reshape(1, D))
    return pl.pallas_call(
        functools.partial(_ple_kernel, sub=sub, tn=tn),
        grid=(T // tm,),
        in_specs=in_specs,
        out_specs=pl.BlockSpec((tm, D), lambda i: (i, 0)),
        out_shape=jax.ShapeDtypeStruct((T, D), F32),
        compiler_params=_params(("parallel",)),
        name="moe_combine_ple",
    )(*args)


def _moe_layer(x, p, w, shared, layer, final_gain=None):
    T, D = x.shape
    hp, ri, rw, cnt = moe_router(x, w["norm_ffn"], w["w_router"], w["b_router"])
    pos, blk_e, blk_valid, n_rows = _dispatch_plan(ri, cnt[:, 0, :MOE_EXPERTS], MOE_ROW_BLOCK)
    xs = sc_scatter_rows(hp, pos, n_rows)
    ys = moe_experts(xs, blk_e, blk_valid, shared["moe_w_gate"], shared["moe_w_up"],
                     shared["moe_w_down"], layer)
    ysel = sc_gather_rows(ys, pos.T.reshape(-1)).reshape(2, T, D // 2)
    return moe_combine_ple(x, ysel, rw, p, w["norm_ple"], shared["ple_w_gate"],
                           shared["ple_w_proj"], layer, final_gain)


def _trunk(x, p, layers, shared, norm_final):
    B, S, D = x.shape
    x = x.reshape(B * S, D)
    p = p.reshape(p.shape[0], B * S, p.shape[-1])
    for i, w in enumerate(layers):
        kind = w["kind"]
        if kind == 0:
            qkv = norm_matmul(x, w["norm_mix"], w["w_qkv"], w["qkv_colscale"])
            o = na_attention(qkv, w["na_bias"], B, S)
            x = matmul_residual(o, w["w_o"], x)
        elif kind == 1:
            x = pool_mixer(x, w["norm_mix"], w["pool_w"], w["pool_scale"], B, S)
        else:
            z = norm_matmul(x, w["norm_mix"], w["w_in"], w["in_colscale"], tn=w["in_tn"])
            a = z[:, 3 * D:3 * D + 2 * GLA_GATE_RANK]
            o = gla_mixer_core(z, a, w["w_a2"], w["b_a"], w["g_norm"], B, S, D)
            x = matmul_residual(o, w["w_o"], x)
        last = i == len(layers) - 1
        x = _moe_layer(x, p, w, shared, i, norm_final if last else None)
    return x.reshape(B, S, D)


def _prepare_layers(norm_mix, norm_ffn, norm_ple, na_w_qkv, na_rpb, na_w_o, pool_w, pool_scale,
                    gla_w_in, gla_w_a2, gla_b_a, gla_norm, gla_w_o, moe_w_rg, moe_b_rg, moe_w_re,
                    moe_b_re, moe_w_gate, moe_w_up, moe_w_down, ple_w_proj, ple_w_gate):
    depth, D = norm_mix.shape
    layers = []
    for i in range(depth):
        j, kind = i // 3, i % 3
        w = {"kind": kind, "norm_mix": norm_mix[i], "norm_ffn": norm_ffn[i], "norm_ple": norm_ple[i]}
        if kind == 0:
            w["w_qkv"] = na_w_qkv[j].astype(BF16)
            w["qkv_colscale"] = jnp.concatenate(
                [jnp.full((D,), NA_HEAD_DIM ** -0.5 * LOG2E, F32), jnp.ones((2 * D,), F32)])
            w["na_bias"] = _na_bias_table(na_rpb[j])
            w["w_o"] = na_w_o[j].astype(BF16)
        elif kind == 1:
            w["pool_w"] = pool_w[j].astype(BF16)
            w["pool_scale"] = pool_scale[j]
        else:
            n_in = gla_w_in.shape[2]
            n_pad = -(-n_in // (7 * LANES)) * (7 * LANES)
            w["w_in"] = jnp.pad(gla_w_in[j], ((0, 0), (0, n_pad - n_in))).astype(BF16)
            dkh = D // 2 // GLA_HEADS
            w["in_colscale"] = jnp.concatenate(
                [jnp.full((D // 2,), dkh ** -0.5, F32), jnp.ones((n_pad - D // 2,), F32)])
            w["in_tn"] = n_pad // 7
            w["w_a2"] = gla_w_a2[j]
            w["b_a"] = gla_b_a[j]
            w["g_norm"] = gla_norm[j]
            w["w_o"] = gla_w_o[j].astype(BF16)
        wr = jnp.concatenate([moe_w_rg[i], moe_w_re[i]], axis=1)
        br = jnp.concatenate([moe_b_rg[i], moe_b_re[i]])
        wr = jnp.pad(wr, ((0, 0), (0, LANES - wr.shape[1])))
        wr_hi = wr.astype(BF16)
        wr_lo = (wr - wr_hi.astype(F32)).astype(BF16)
        w["w_router"] = jnp.concatenate([wr_hi, wr_lo], axis=1)
        w["b_router"] = jnp.pad(br, (0, LANES - br.shape[0])).reshape(1, LANES)
        layers.append(w)
    shared = {"moe_w_gate": moe_w_gate.astype(BF16), "moe_w_up": moe_w_up.astype(BF16),
              "moe_w_down": moe_w_down.astype(BF16), "ple_w_gate": ple_w_gate.astype(BF16),
              "ple_w_proj": ple_w_proj.astype(BF16)}
    return layers, shared


def kernel(x_prompt, x_sample, p_prompt, p_sample, norm_mix, norm_ffn, norm_ple, norm_final, na_w_qkv, na_rpb, na_w_o, pool_w, pool_scale, gla_w_in, gla_w_a2, gla_b_a, gla_norm, gla_w_o, moe_w_rg, moe_b_rg, moe_w_re, moe_b_re, moe_w_gate, moe_w_up, moe_w_down, ple_w_proj, ple_w_gate):
    layers, shared = _prepare_layers(
        norm_mix, norm_ffn, norm_ple, na_w_qkv, na_rpb, na_w_o, pool_w, pool_scale, gla_w_in, gla_w_a2,
        gla_b_a, gla_norm, gla_w_o, moe_w_rg, moe_b_rg, moe_w_re, moe_b_re, moe_w_gate, moe_w_up,
        moe_w_down, ple_w_proj, ple_w_gate)
    y_prompt = _trunk(x_prompt, p_prompt, layers, shared, norm_final)
    y_sample = _trunk(x_sample, p_sample, layers, shared, norm_final)
    return (y_prompt, y_sample)
```
